```python
import jax, jax.numpy as jnp
from jax import lax
import numpy as np

D_MODEL = 2048
BATCH = 4
SEQ = 2048
DEPTH = 2

D_PLE = 256
HEAD_DIM = 128
D_MIX = D_MODEL
D_POOL = D_MIX // 4
D_CONV = (D_MIX - D_POOL) // 2
D_SGU = D_MIX - D_POOL - D_CONV
N_HEADS_CONV = D_CONV // HEAD_DIM
N_HEADS_SGU = D_SGU // HEAD_DIM
POOL_WINDOWS = (2, 4, 8, 16)
N_GROUPS_POOL = len(POOL_WINDOWS)
POOL_GROUP_DIM = D_POOL // N_GROUPS_POOL
D_IN = 3 * D_CONV + D_POOL + 2 * D_SGU
CONV_WIDTH = 3
CHUNK = 128
N_EXPERTS = 32
TOP_K = 4
D_FF = D_MODEL
SWIGLU_LIMIT = 7.0
SWIGLU_ALPHA = 1.702
EPS = 1e-6

kernel_name = "hybrid_conv_pool_sgu_moe_ple"


def _rms(x):
    xf = x.astype(jnp.float32)
    return xf * lax.rsqrt(jnp.mean(xf * xf, axis=-1, keepdims=True) + EPS)


def rmsnorm(x, g):
    return (_rms(x) * g.astype(jnp.float32)).astype(x.dtype)


def layernorm(x, g, b):
    xf = x.astype(jnp.float32)
    mu = jnp.mean(xf, axis=-1, keepdims=True)
    xc = xf - mu
    var = jnp.mean(xc * xc, axis=-1, keepdims=True)
    y = xc * lax.rsqrt(var + EPS) * g.astype(jnp.float32) + b.astype(jnp.float32)
    return y.astype(x.dtype)


def short_conv_mixer(z, b_gate, c_gate, conv_w):
    zc = c_gate * z
    y = lax.conv_general_dilated(
        zc, conv_w[:, None, :].astype(zc.dtype), window_strides=(1,),
        padding=[(CONV_WIDTH - 1, 0)], dimension_numbers=("NWC", "WIO", "NWC"),
        feature_group_count=D_CONV)
    return b_gate * y


def pool_mixer(z, w_pool, pool_scale):
    b, s, _ = z.shape
    zg = z.reshape(b, s, N_GROUPS_POOL, POOL_GROUP_DIM).astype(jnp.float32)
    cs = jnp.pad(jnp.cumsum(zg, axis=1), ((0, 0), (1, 0), (0, 0), (0, 0)))
    t = jnp.arange(1, s + 1, dtype=jnp.int32)[:, None]
    win = jnp.array(POOL_WINDOWS, dtype=jnp.int32)[None, :]
    lo = jnp.maximum(t - win, 0)
    g_idx = jnp.arange(N_GROUPS_POOL, dtype=jnp.int32)[None, :]
    cs_lo = cs[:, lo, g_idx, :]
    count = (t - lo).astype(jnp.float32)[None, :, :, None]
    pooled = (cs[:, 1:] - cs_lo) / count - zg
    y = jnp.einsum("bsgc,gcd->bsgd", pooled.astype(z.dtype), w_pool)
    return y.reshape(b, s, D_POOL) * pool_scale


def sgu_mixer(u, v, ln_g, ln_b, w_spatial, b_spatial):
    b, s, _ = u.shape
    vn = layernorm(v, ln_g, ln_b)
    vc = vn.reshape(b, s // CHUNK, CHUNK, N_HEADS_SGU, HEAD_DIM)
    mask = jnp.tril(jnp.ones((CHUNK, CHUNK), dtype=bool))
    ws = jnp.where(mask[None], w_spatial, 0.0)
    y = jnp.einsum("hts,bnshd->bnthd", ws, vc) + b_spatial.T[:, :, None]
    return u * y.reshape(b, s, D_SGU)


def moe_ffn(x, w_router, b_router, w_gate, b_gate, w_up, b_up, w_down, b_down):
    b, s, d = x.shape
    xf = x.reshape(b * s, d)
    logits = (xf @ w_router).astype(jnp.float32) + b_router.astype(jnp.float32)
    top_val, top_idx = lax.top_k(logits, TOP_K)
    probs = jax.nn.softmax(top_val, axis=-1)
    gates = jnp.einsum("nk,nke->ne", probs,
                       jax.nn.one_hot(top_idx, N_EXPERTS, dtype=jnp.float32)).astype(x.dtype)
    out = jnp.zeros_like(xf)
    for e in range(N_EXPERTS):
        g = jnp.minimum(xf @ w_gate[e] + b_gate[e], SWIGLU_LIMIT)
        up = jnp.clip(xf @ w_up[e] + b_up[e], -SWIGLU_LIMIT, SWIGLU_LIMIT)
        act = (up + 1.0) * g * jax.nn.sigmoid(SWIGLU_ALPHA * g)
        out = out + gates[:, e:e + 1] * (act @ w_down[e] + b_down[e])
    return out.reshape(b, s, d)


def setup_inputs(seed: int = 0) -> dict:
    key = jax.random.key(seed)
    ks = jax.random.split(key, 28)
    f32 = jnp.float32

    def nrm(k, shape, scale):
        return jax.random.normal(k, shape, f32) * scale

    def gain(k, shape):
        return 1.0 + 0.05 * jax.random.normal(k, shape, f32)

    return {
        "x": nrm(ks[0], (BATCH, SEQ, D_MODEL), 1.0),
        "p": nrm(ks[1], (DEPTH, BATCH, SEQ, D_PLE), 1.0),
        "g_mix_norm": gain(ks[2], (DEPTH, D_MODEL)),
        "w_in": nrm(ks[3], (DEPTH, D_MODEL, D_IN), D_MODEL ** -0.5),
        "conv_w": nrm(ks[4], (DEPTH, CONV_WIDTH, D_CONV), CONV_WIDTH ** -0.5),
        "w_pool": nrm(ks[5], (DEPTH, N_GROUPS_POOL, POOL_GROUP_DIM, POOL_GROUP_DIM), POOL_GROUP_DIM ** -0.5),
        "pool_scale": gain(ks[6], (DEPTH, D_POOL)),
        "ln_v_g": gain(ks[7], (DEPTH, D_SGU)),
        "ln_v_b": nrm(ks[8], (DEPTH, D_SGU), 0.02),
        "w_spatial": nrm(ks[9], (DEPTH, N_HEADS_SGU, CHUNK, CHUNK), CHUNK ** -0.5),
        "b_spatial": 1.0 + nrm(ks[10], (DEPTH, N_HEADS_SGU, CHUNK), 0.1),
        "g_out": gain(ks[11], (DEPTH, D_MIX)),
        "w_out": nrm(ks[12], (DEPTH, D_MIX, D_MODEL), D_MIX ** -0.5),
        "g_ffn_norm": gain(ks[13], (DEPTH, D_MODEL)),
        "w_router": nrm(ks[14], (DEPTH, D_MODEL, N_EXPERTS), D_MODEL ** -0.5),
        "b_router": nrm(ks[15], (DEPTH, N_EXPERTS), 0.01),
        "w_gate": nrm(ks[16], (DEPTH, N_EXPERTS, D_MODEL, D_FF), D_MODEL ** -0.5),
        "b_gate": nrm(ks[17], (DEPTH, N_EXPERTS, D_FF), 0.02),
        "w_up": nrm(ks[18], (DEPTH, N_EXPERTS, D_MODEL, D_FF), D_MODEL ** -0.5),
        "b_up": nrm(ks[19], (DEPTH, N_EXPERTS, D_FF), 0.02),
        "w_down": nrm(ks[20], (DEPTH, N_EXPERTS, D_FF, D_MODEL), D_FF ** -0.5),
        "b_down": nrm(ks[21], (DEPTH, N_EXPERTS, D_MODEL), 0.02),
        "g_ple_in": gain(ks[22], (DEPTH, D_MODEL)),
        "w_ple_gate": nrm(ks[23], (DEPTH, D_MODEL, D_MODEL), D_MODEL ** -0.5),
        "w_ple_proj": nrm(ks[24], (DEPTH, D_PLE, D_MODEL), D_PLE ** -0.5),
        "g_ple_post": gain(ks[25], (DEPTH, D_MODEL)),
        "g_final": gain(ks[26], (D_MODEL,)),
    }


def reference(x, p, g_mix_norm, w_in, conv_w, w_pool, pool_scale, ln_v_g, ln_v_b,
              w_spatial, b_spatial, g_out, w_out, g_ffn_norm, w_router, b_router,
              w_gate, b_gate, w_up, b_up, w_down, b_down, g_ple_in, w_ple_gate,
              w_ple_proj, g_ple_post, g_final):
    splits = [D_CONV, 2 * D_CONV, 3 * D_CONV, 3 * D_CONV + D_POOL, 3 * D_CONV + D_POOL + D_SGU]
    h = x
    for i in range(DEPTH):
        hn = rmsnorm(h, g_mix_norm[i])
        proj = hn @ w_in[i]
        z_a, b_a, c_a, z_b, u_c, v_c = jnp.split(proj, splits, axis=-1)
        y_a = short_conv_mixer(z_a, b_a, c_a, conv_w[i])
        y_b = pool_mixer(z_b, w_pool[i], pool_scale[i])
        y_c = sgu_mixer(u_c, v_c, ln_v_g[i], ln_v_b[i], w_spatial[i], b_spatial[i])
        y = jnp.concatenate([_rms(y_a), _rms(y_b), _rms(y_c)], axis=-1) * g_out[i].astype(jnp.float32)
        h = h + y.astype(h.dtype) @ w_out[i]
        h = h + moe_ffn(rmsnorm(h, g_ffn_norm[i]), w_router[i], b_router[i], w_gate[i], b_gate[i],
                        w_up[i], b_up[i], w_down[i], b_down[i])
        gate = jax.nn.sigmoid(rmsnorm(h, g_ple_in[i]) @ w_ple_gate[i])
        ple = rmsnorm(p[i] @ w_ple_proj[i], g_ple_post[i])
        h = h + gate * ple
    return rmsnorm(h, g_final)
```

```python
import functools

import jax
import jax.numpy as jnp
from jax import lax
from jax.experimental import pallas as pl
from jax.experimental.pallas import tpu as pltpu

F32 = jnp.float32
BF16 = jnp.bfloat16

EPS = 1e-6
HEAD_DIM = 128
CHUNK = 128
CONV_WIDTH = 3
POOL_WINDOWS = (2, 4, 8, 16)
TOP_K = 4
SWIGLU_LIMIT = 7.0
SWIGLU_ALPHA = 1.702

HALO = 16
ROUTER_LANES = 128
VMEM_LIMIT = 56 * 1024 * 1024

SUB = 256
TILE_SUBS = 4
FF_BLOCK = 256


def _rms(x):
    return x * lax.rsqrt(jnp.mean(x * x, axis=-1, keepdims=True) + EPS)


def _params(semantics):
    return pltpu.CompilerParams(dimension_semantics=semantics, vmem_limit_bytes=VMEM_LIMIT)


def _inproj_body(h_ref, g_ref, w_ref, o_ref):
    xn = _rms(h_ref[...]) * g_ref[...]
    o_ref[...] = jnp.dot(xn.astype(BF16), w_ref[...], preferred_element_type=F32).astype(o_ref.dtype)


def _inproj(h, g, w_bf16, *, tm, tn):
    n, d = h.shape
    d_in = w_bf16.shape[1]
    return pl.pallas_call(
        _inproj_body,
        grid=(d_in // tn, n // tm),
        in_specs=[
            pl.BlockSpec((tm, d), lambda j, i: (i, 0)),
            pl.BlockSpec((1, d), lambda j, i: (0, 0)),
            pl.BlockSpec((d, tn), lambda j, i: (0, j)),
        ],
        out_specs=pl.BlockSpec((tm, tn), lambda j, i: (i, j)),
        out_shape=jax.ShapeDtypeStruct((n, d_in), BF16),
        compiler_params=_params(("arbitrary", "arbitrary")),
        name="inproj",
    )(h, g, w_bf16)


def _mixer_body(proj_ref, halo_ref, convw_ref, wpool_ref, pscale_ref, lng_ref, lnb_ref,
                wsp_ref, bsp_ref, gout_ref, o_ref, za_scr, zb_scr, *, tiles_per_seq, d_conv, d_pool, d_sgu):
    t_rows = proj_ref.shape[0]
    i = pl.program_id(0)
    seq_tile = i % tiles_per_seq
    keep = (seq_tile > 0).astype(F32)

    o_za, o_ba, o_ca = 0, d_conv, 2 * d_conv
    o_zb = 3 * d_conv
    o_u = o_zb + d_pool
    o_v = o_u + d_sgu

    zc = proj_ref[:, o_ca:o_ca + d_conv].astype(F32) * proj_ref[:, o_za:o_za + d_conv].astype(F32)
    zc_halo = (halo_ref[:, o_ca:o_ca + d_conv].astype(F32) * halo_ref[:, o_za:o_za + d_conv].astype(F32)) * keep
    za_scr[0:HALO, :] = zc_halo
    za_scr[HALO:HALO + t_rows, :] = zc
    conv = zc * convw_ref[CONV_WIDTH - 1:CONV_WIDTH, :]
    for k in range(1, CONV_WIDTH):
        conv = conv + za_scr[HALO - k:HALO - k + t_rows, :] * convw_ref[CONV_WIDTH - 1 - k:CONV_WIDTH - k, :]
    y_a = proj_ref[:, o_ba:o_ba + d_conv].astype(F32) * conv

    zb = proj_ref[:, o_zb:o_zb + d_pool].astype(F32)
    zb_scr[0:HALO, :] = halo_ref[:, o_zb:o_zb + d_pool].astype(F32) * keep
    zb_scr[HALO:HALO + t_rows, :] = zb
    t_pos = seq_tile * t_rows + lax.broadcasted_iota(jnp.int32, (t_rows, 1), 0) + 1
    gdim = d_pool // len(POOL_WINDOWS)
    yb_parts = []
    for g, win in enumerate(POOL_WINDOWS):
        c0 = g * gdim
        s = zb[:, c0:c0 + gdim]
        for k in range(1, win):
            s = s + zb_scr[HALO - k:HALO - k + t_rows, c0:c0 + gdim]
        count = jnp.minimum(t_pos, win).astype(F32)
        pooled = s / count - zb[:, c0:c0 + gdim]
        yb_parts.append(jnp.dot(pooled.astype(BF16), wpool_ref[g].astype(BF16), preferred_element_type=F32))
    y_b = jnp.concatenate(yb_parts, axis=-1) * pscale_ref[...]

    v = proj_ref[:, o_v:o_v + d_sgu].astype(F32)
    mu = jnp.mean(v, axis=-1, keepdims=True)
    vc = v - mu
    var = jnp.mean(vc * vc, axis=-1, keepdims=True)
    vn = (vc * lax.rsqrt(var + EPS) * lng_ref[...] + lnb_ref[...]).astype(BF16)
    n_heads = d_sgu // HEAD_DIM
    row = lax.broadcasted_iota(jnp.int32, (CHUNK, CHUNK), 0)
    col = lax.broadcasted_iota(jnp.int32, (CHUNK, CHUNK), 1)
    yc_rows = []
    for c in range(t_rows // CHUNK):
        heads = []
        for hd in range(n_heads):
            ws = jnp.where(row >= col, wsp_ref[hd], 0.0).astype(BF16)
            vch = vn[c * CHUNK:(c + 1) * CHUNK, hd * HEAD_DIM:(hd + 1) * HEAD_DIM]
            heads.append(jnp.dot(ws, vch, preferred_element_type=F32) + bsp_ref[:, hd:hd + 1])
        yc_rows.append(jnp.concatenate(heads, axis=-1))
    y_c = proj_ref[:, o_u:o_u + d_sgu].astype(F32) * jnp.concatenate(yc_rows, axis=0)

    o_ref[:, 0:d_conv] = (_rms(y_a) * gout_ref[:, 0:d_conv]).astype(o_ref.dtype)
    o_ref[:, d_conv:d_conv + d_pool] = (_rms(y_b) * gout_ref[:, d_conv:d_conv + d_pool]).astype(o_ref.dtype)
    o_ref[:, d_conv + d_pool:] = (_rms(y_c) * gout_ref[:, d_conv + d_pool:]).astype(o_ref.dtype)


def _mixers(proj, conv_w, w_pool, pool_scale, ln_g, ln_b, w_spatial, b_spatial_t, g_out, *, seq, t_rows):
    n, d_in = proj.shape
    d_conv = conv_w.shape[1]
    d_pool = pool_scale.shape[1]
    d_sgu = ln_g.shape[1]
    d_mix = g_out.shape[1]
    halo_blocks = t_rows // HALO
    whole = lambda a: pl.BlockSpec(a.shape, lambda i: (0,) * a.ndim)
    body = functools.partial(_mixer_body, tiles_per_seq=seq // t_rows, d_conv=d_conv, d_pool=d_pool, d_sgu=d_sgu)
    return pl.pallas_call(
        body,
        grid=(n // t_rows,),
        in_specs=[
            pl.BlockSpec((t_rows, d_in), lambda i: (i, 0)),
            pl.BlockSpec((HALO, d_in), lambda i: (jnp.maximum(i * halo_blocks - 1, 0), 0)),
            whole(conv_w), whole(w_pool), whole(pool_scale), whole(ln_g), whole(ln_b),
            whole(w_spatial), whole(b_spatial_t), whole(g_out),
        ],
        out_specs=pl.BlockSpec((t_rows, d_mix), lambda i: (i, 0)),
        out_shape=jax.ShapeDtypeStruct((n, d_mix), BF16),
        scratch_shapes=[pltpu.VMEM((HALO + t_rows, d_conv), F32), pltpu.VMEM((HALO + t_rows, d_pool), F32)],
        compiler_params=_params(("arbitrary",)),
        name="mixers",
    )(proj, proj, conv_w, w_pool, pool_scale, ln_g, ln_b, w_spatial, b_spatial_t, g_out)


def _outproj_router_body(y_ref, h_ref, w_ref, g_ref, wr_ref, br_ref, h1_ref, xn_ref, idx_ref, prob_ref):
    h1 = h_ref[...] + jnp.dot(y_ref[...], w_ref[...], preferred_element_type=F32)
    h1_ref[...] = h1
    xn = _rms(h1) * g_ref[...]
    xn_ref[...] = xn
    logits = jnp.dot(xn, wr_ref[...], preferred_element_type=F32, precision=lax.Precision.HIGHEST) + br_ref[...]
    lane = lax.broadcasted_iota(jnp.int32, logits.shape, 1)
    vals, idxs = [], []
    for _ in range(TOP_K):
        m = jnp.max(logits, axis=-1, keepdims=True)
        sel = jnp.min(jnp.where(logits == m, lane, ROUTER_LANES), axis=-1, keepdims=True)
        vals.append(m)
        idxs.append(sel)
        logits = jnp.where(lane == sel, -jnp.inf, logits)
    exps = [jnp.exp(v - vals[0]) for v in vals]
    denom = exps[0]
    for e in exps[1:]:
        denom = denom + e
    idx_out = jnp.zeros(lane.shape, jnp.int32)
    prob_out = jnp.zeros(lane.shape, F32)
    for k in range(TOP_K):
        idx_out = jnp.where(lane == k, idxs[k], idx_out)
        prob_out = jnp.where(lane == k, exps[k] / denom, prob_out)
    idx_ref[...] = idx_out
    prob_ref[...] = prob_out


def _outproj_router(y, h, w_out_bf16, g_ffn, w_router_pad, b_router_pad, *, tm):
    n, d = h.shape
    whole = lambda a: pl.BlockSpec(a.shape, lambda i: (0,) * a.ndim)
    row_block = lambda width: pl.BlockSpec((tm, width), lambda i: (i, 0))
    return pl.pallas_call(
        _outproj_router_body,
        grid=(n // tm,),
        in_specs=[row_block(y.shape[1]), row_block(d), whole(w_out_bf16), whole(g_ffn),
                  whole(w_router_pad), whole(b_router_pad)],
        out_specs=[row_block(d), row_block(d), row_block(ROUTER_LANES), row_block(ROUTER_LANES)],
        out_shape=[jax.ShapeDtypeStruct((n, d), F32), jax.ShapeDtypeStruct((n, d), F32),
                   jax.ShapeDtypeStruct((n, ROUTER_LANES), jnp.int32),
                   jax.ShapeDtypeStruct((n, ROUTER_LANES), F32)],
        compiler_params=_params(("arbitrary",)),
        name="outproj_router",
    )(y, h, w_out_bf16, g_ffn, w_router_pad, b_router_pad)


def _routing_plan(top_idx, n_experts):
    n = top_idx.shape[0]
    tile_rows = SUB * TILE_SUBS
    max_subs = (n * TOP_K) // SUB + n_experts
    max_tiles = max_subs // TILE_SUBS + n_experts
    hit = (top_idx[:, :, None] == jnp.arange(n_experts, dtype=jnp.int32)).any(axis=1).astype(jnp.int32)
    before = jnp.cumsum(hit, axis=0) - hit
    counts = jnp.sum(hit, axis=0)
    rank = jnp.take_along_axis(before, top_idx, axis=1)
    nsub = (counts + SUB - 1) // SUB
    sub_end = jnp.cumsum(nsub)
    sub_off = sub_end - nsub
    pos = sub_off[top_idx] * SUB + rank
    token = jnp.broadcast_to(jnp.arange(n, dtype=jnp.int32)[:, None], pos.shape)
    src = jnp.zeros((max_subs * SUB,), jnp.int32).at[pos.reshape(-1)].set(token.reshape(-1))
    ntile = (nsub + TILE_SUBS - 1) // TILE_SUBS
    tile_end = jnp.cumsum(ntile)
    tile_off = tile_end - ntile
    n_tiles = tile_end[-1]
    t = jnp.arange(max_tiles, dtype=jnp.int32)
    t_eff = jnp.minimum(t, n_tiles - 1)
    expert = jnp.searchsorted(tile_end, t_eff, side="right").astype(jnp.int32)
    within = t_eff - tile_off[expert]
    start_sub = sub_off[expert] + within * TILE_SUBS
    tile_nsub = jnp.where(t < n_tiles, jnp.minimum(TILE_SUBS, nsub[expert] - within * TILE_SUBS), 0)
    rows = jnp.minimum(start_sub[:, None] * SUB + jnp.arange(tile_rows, dtype=jnp.int32)[None, :],
                       max_subs * SUB - 1)
    tile_src = src[rows].reshape(max_tiles, 1, tile_rows)
    return dict(pos=pos, tile_src=tile_src, tile_expert=expert, tile_nsub=tile_nsub.astype(jnp.int32),
                used_subs=sub_end[-1:].astype(jnp.int32),
                tile_start=start_sub.astype(jnp.int32), n_rows=max_subs * SUB)


def _row_copy(src_hbm, dst_vmem, sem, src_row, dst_row):
    return pltpu.make_async_copy(src_hbm.at[pl.ds(src_row, 1)], dst_vmem.at[pl.ds(dst_row, 1)], sem)


def _moe_body(expert_ref, nsub_ref, start_ref, used_ref, src_ref, x_hbm, wg_ref, bg_ref, wu_ref, bu_ref, wd_ref,
              bd_ref, ys_hbm, xg, xb, acc, wgb, wub, wdb, gsem, osem):
    del expert_ref
    t = pl.program_id(0)
    f = pl.program_id(1)
    n_f = pl.num_programs(1)
    ns = nsub_ref[t]

    def sub_rows(j):
        return pl.ds(pl.multiple_of(j * SUB, SUB), SUB)

    @pl.when((t == 0) & (f == 0))
    def _():
        used = used_ref[0]
        n_tail = ys_hbm.shape[0] // SUB - used
        xg[0:SUB, :] = jnp.zeros((SUB, xg.shape[1]), F32)

        def tail_copy(j):
            dst = pl.ds(pl.multiple_of((used + j) * SUB, SUB), SUB)
            return pltpu.make_async_copy(xg.at[pl.ds(0, SUB)], ys_hbm.at[dst], osem)

        def push(j, carry):
            tail_copy(j).start()
            return carry
        lax.fori_loop(0, n_tail, push, 0)

        def drain(j, carry):
            tail_copy(j).wait()
            return carry
        lax.fori_loop(0, n_tail, drain, 0)

    @pl.when(ns > 0)
    def _():
        @pl.when(f == 0)
        def _():
            def issue(r, carry):
                _row_copy(x_hbm, xg, gsem, src_ref[0, r], r).start()
                return carry
            lax.fori_loop(0, ns * SUB, issue, 0)

            def land(j, carry):
                pltpu.make_async_copy(x_hbm.at[pl.ds(0, SUB)], xg.at[sub_rows(j)], gsem).wait()
                return carry
            lax.fori_loop(0, ns, land, 0)

            def narrow(j, carry):
                xb[sub_rows(j), :] = xg[sub_rows(j), :].astype(BF16)
                return carry
            lax.fori_loop(0, ns, narrow, 0)

        wgb[...] = wg_ref[...].astype(BF16)
        wub[...] = wu_ref[...].astype(BF16)
        wdb[...] = wd_ref[...].astype(BF16)

        def ffn(j, carry):
            x = xb[sub_rows(j), :]
            g = jnp.minimum(jnp.dot(x, wgb[...], preferred_element_type=F32) + bg_ref[...], SWIGLU_LIMIT)
            u = jnp.clip(jnp.dot(x, wub[...], preferred_element_type=F32) + bu_ref[...], -SWIGLU_LIMIT, SWIGLU_LIMIT)
            a = (u + 1.0) * g * jax.nn.sigmoid(SWIGLU_ALPHA * g)
            part = jnp.dot(a.astype(BF16), wdb[...], preferred_element_type=F32)

            @pl.when(f == 0)
            def _():
                acc[sub_rows(j), :] = part + bd_ref[...]

            @pl.when(f > 0)
            def _():
                acc[sub_rows(j), :] += part
            return carry
        lax.fori_loop(0, ns, ffn, 0)

        @pl.when(f == n_f - 1)
        def _():
            def out_copy(j):
                dst = pl.ds(pl.multiple_of((start_ref[t] + j) * SUB, SUB), SUB)
                return pltpu.make_async_copy(acc.at[sub_rows(j)], ys_hbm.at[dst], osem)

            def push(j, carry):
                out_copy(j).start()
                return carry
            lax.fori_loop(0, ns, push, 0)

            def drain(j, carry):
                out_copy(j).wait()
                return carry
            lax.fori_loop(0, ns, drain, 0)


def _moe_ffn(xn, plan, w_gate, b_gate, w_up, b_up, w_down, b_down):
    n, d = xn.shape
    n_experts, _, d_ff = w_gate.shape
    tile_rows = SUB * TILE_SUBS
    max_tiles = plan["tile_expert"].shape[0]
    n_f = d_ff // FF_BLOCK

    def ff_block(f, nsub_ref, t):
        return jnp.where(nsub_ref[t] > 0, f, n_f - 1)

    col_w = pl.BlockSpec((None, d, FF_BLOCK), lambda t, f, e, ns, st, used: (e[t], 0, ff_block(f, ns, t)))
    col_b = pl.BlockSpec((None, 1, FF_BLOCK), lambda t, f, e, ns, st, used: (e[t], 0, ff_block(f, ns, t)))
    grid_spec = pltpu.PrefetchScalarGridSpec(
        num_scalar_prefetch=4,
        grid=(max_tiles, n_f),
        in_specs=[
            pl.BlockSpec((None, 1, tile_rows), lambda t, f, e, ns, st, used: (t, 0, 0), memory_space=pltpu.SMEM),
            pl.BlockSpec(memory_space=pl.ANY),
            col_w, col_b, col_w, col_b,
            pl.BlockSpec((None, FF_BLOCK, d), lambda t, f, e, ns, st, used: (e[t], ff_block(f, ns, t), 0)),
            pl.BlockSpec((None, 1, d), lambda t, f, e, ns, st, used: (e[t], 0, 0)),
        ],
        out_specs=pl.BlockSpec(memory_space=pl.ANY),
        scratch_shapes=[
            pltpu.VMEM((tile_rows, d), F32), pltpu.VMEM((tile_rows, d), BF16), pltpu.VMEM((tile_rows, d), F32),
            pltpu.VMEM((d, FF_BLOCK), BF16), pltpu.VMEM((d, FF_BLOCK), BF16), pltpu.VMEM((FF_BLOCK, d), BF16),
            pltpu.SemaphoreType.DMA, pltpu.SemaphoreType.DMA,
        ],
    )
    return pl.pallas_call(
        _moe_body,
        grid_spec=grid_spec,
        out_shape=jax.ShapeDtypeStruct((plan["n_rows"], d), F32),
        compiler_params=_params(("arbitrary", "arbitrary")),
        name="moe_ffn",
    )(plan["tile_expert"], plan["tile_nsub"], plan["tile_start"], plan["used_subs"], plan["tile_src"], xn,
      w_gate, b_gate.reshape(n_experts, 1, d_ff), w_up, b_up.reshape(n_experts, 1, d_ff),
      w_down, b_down.reshape(n_experts, 1, d))


def _ple_body(pos_ref, prob_ref, h_ref, ys_hbm, p_ref, gin_ref, wg_ref, wp_ref, gpost_ref, gfin_ref, o_ref,
              yg, sem, *, final):
    tp = h_ref.shape[0]

    def issue(r, carry):
        for k in range(TOP_K):
            _row_copy(ys_hbm, yg.at[k], sem, pos_ref[0, k * tp + r], r).start()
        return carry
    lax.fori_loop(0, tp, issue, 0)
    for k in range(TOP_K):
        pltpu.make_async_copy(ys_hbm.at[pl.ds(0, tp)], yg.at[k], sem).wait()

    h2 = h_ref[...]
    for k in range(TOP_K):
        h2 = h2 + prob_ref[:, k:k + 1] * yg[k]
    xn = (_rms(h2) * gin_ref[...]).astype(BF16)
    gate = jax.nn.sigmoid(jnp.dot(xn, wg_ref[...], preferred_element_type=F32))
    ple = _rms(jnp.dot(p_ref[...].astype(BF16), wp_ref[...], preferred_element_type=F32)) * gpost_ref[...]
    h3 = h2 + gate * ple
    if final:
        h3 = _rms(h3) * gfin_ref[...]
    o_ref[...] = h3


def _combine_ple(pos, probs, h1, ys, p, g_in, w_gate_bf16, w_proj_bf16, g_post, g_final, *, tp, final):
    n, d = h1.shape
    d_ple = p.shape[1]
    pos_tiles = pos.reshape(n // tp, tp, TOP_K).transpose(0, 2, 1).reshape(n // tp, 1, TOP_K * tp)
    whole = lambda a: pl.BlockSpec(a.shape, lambda i: (0,) * a.ndim)
    row_block = lambda width: pl.BlockSpec((tp, width), lambda i: (i, 0))
    return pl.pallas_call(
        functools.partial(_ple_body, final=final),
        grid=(n // tp,),
        in_specs=[
            pl.BlockSpec((None, 1, TOP_K * tp), lambda i: (i, 0, 0), memory_space=pltpu.SMEM),
            row_block(TOP_K), row_block(d), pl.BlockSpec(memory_space=pl.ANY), row_block(d_ple),
            whole(g_in), whole(w_gate_bf16), whole(w_proj_bf16), whole(g_post), whole(g_final),
        ],
        out_specs=row_block(d),
        out_shape=jax.ShapeDtypeStruct((n, d), F32),
        scratch_shapes=[pltpu.VMEM((TOP_K, tp, d), F32), pltpu.SemaphoreType.DMA],
        compiler_params=_params(("arbitrary",)),
        name="combine_ple",
    )(pos_tiles, probs, h1, ys, p, g_in, w_gate_bf16, w_proj_bf16, g_post, g_final)


def kernel(x, p, g_mix_norm, w_in, conv_w, w_pool, pool_scale, ln_v_g, ln_v_b, w_spatial, b_spatial, g_out,
           w_out, g_ffn_norm, w_router, b_router, w_gate, b_gate, w_up, b_up, w_down, b_down, g_ple_in,
           w_ple_gate, w_ple_proj, g_ple_post, g_final):
    batch, seq, d = x.shape
    depth = w_in.shape[0]
    n = batch * seq
    n_experts = w_router.shape[2]
    row = lambda v: v.reshape(1, -1)

    h = x.reshape(n, d)
    for i in range(depth):
        proj = _inproj(h, row(g_mix_norm[i]), w_in[i].astype(BF16), tm=512, tn=w_in.shape[2] // 2)
        y = _mixers(proj, conv_w[i], w_pool[i], row(pool_scale[i]), row(ln_v_g[i]), row(ln_v_b[i]),
                    w_spatial[i], b_spatial[i].T, row(g_out[i]), seq=seq, t_rows=256)
        w_router_pad = jnp.pad(w_router[i], ((0, 0), (0, ROUTER_LANES - n_experts)))
        b_router_pad = jnp.pad(row(b_router[i]), ((0, 0), (0, ROUTER_LANES - n_experts)), constant_values=-jnp.inf)
        h1, xn, idx_pad, prob_pad = _outproj_router(y, h, w_out[i].astype(BF16), row(g_ffn_norm[i]),
                                                    w_router_pad, b_router_pad, tm=256)
        top_idx, probs = idx_pad[:, :TOP_K], prob_pad[:, :TOP_K]
        plan = _routing_plan(top_idx, n_experts)
        ys = _moe_ffn(xn, plan, w_gate[i], b_gate[i], w_up[i], b_up[i], w_down[i], b_down[i])
        h = _combine_ple(plan["pos"], probs, h1, ys, p[i].reshape(n, -1), row(g_ple_in[i]),
                         w_ple_gate[i].astype(BF16), w_ple_proj[i].astype(BF16), row(g_ple_post[i]),
                         row(g_final), tp=256, final=(i == depth - 1))
    return h.reshape(batch, seq, d)
```

```python
import functools

import jax
import jax.numpy as jnp
from jax import lax
from jax.experimental import pallas as pl
from jax.experimental.pallas import tpu as pltpu

F32 = jnp.float32
BF16 = jnp.bfloat16

EPS = 1e-6
HEAD_DIM = 128
CHUNK = 128
CONV_WIDTH = 3
POOL_WINDOWS = (2, 4, 8, 16)
TOP_K = 4
SWIGLU_LIMIT = 7.0
SWIGLU_ALPHA = 1.702

HALO = 16
ROUTER_LANES = 128
VMEM_LIMIT = 56 * 1024 * 1024

SUB = 256
TILE_SUBS = 4
FF_BLOCK = 256
ISSUE_UNROLL = 8


def _rms(x):
    return x * lax.rsqrt(jnp.mean(x * x, axis=-1, keepdims=True) + EPS)


def _params(semantics):
    return pltpu.CompilerParams(dimension_semantics=semantics, vmem_limit_bytes=VMEM_LIMIT)


def _inproj_body(h_ref, g_ref, w_ref, o_ref):
    xn = _rms(h_ref[...]) * g_ref[...]
    o_ref[...] = jnp.dot(xn.astype(BF16), w_ref[...], preferred_element_type=F32).astype(o_ref.dtype)


def _inproj(h, g, w_bf16, *, tm, tn):
    n, d = h.shape
    d_in = w_bf16.shape[1]
    return pl.pallas_call(
        _inproj_body,
        grid=(d_in // tn, n // tm),
        in_specs=[
            pl.BlockSpec((tm, d), lambda j, i: (i, 0)),
            pl.BlockSpec((1, d), lambda j, i: (0, 0)),
            pl.BlockSpec((d, tn), lambda j, i: (0, j)),
        ],
        out_specs=pl.BlockSpec((tm, tn), lambda j, i: (i, j)),
        out_shape=jax.ShapeDtypeStruct((n, d_in), BF16),
        compiler_params=_params(("arbitrary", "arbitrary")),
        name="inproj",
    )(h, g, w_bf16)


def _mixer_body(proj_ref, halo_ref, convw_ref, wpool_ref, pscale_ref, lng_ref, lnb_ref,
                wsp_ref, bsp_ref, gout_ref, o_ref, za_scr, zb_scr, *, tiles_per_seq, d_conv, d_pool, d_sgu):
    t_rows = proj_ref.shape[0]
    i = pl.program_id(0)
    seq_tile = i % tiles_per_seq
    keep = (seq_tile > 0).astype(F32)

    o_za, o_ba, o_ca = 0, d_conv, 2 * d_conv
    o_zb = 3 * d_conv
    o_u = o_zb + d_pool
    o_v = o_u + d_sgu

    zc = proj_ref[:, o_ca:o_ca + d_conv].astype(F32) * proj_ref[:, o_za:o_za + d_conv].astype(F32)
    zc_halo = (halo_ref[:, o_ca:o_ca + d_conv].astype(F32) * halo_ref[:, o_za:o_za + d_conv].astype(F32)) * keep
    za_scr[0:HALO, :] = zc_halo
    za_scr[HALO:HALO + t_rows, :] = zc
    conv = zc * convw_ref[CONV_WIDTH - 1:CONV_WIDTH, :]
    for k in range(1, CONV_WIDTH):
        conv = conv + za_scr[HALO - k:HALO - k + t_rows, :] * convw_ref[CONV_WIDTH - 1 - k:CONV_WIDTH - k, :]
    y_a = proj_ref[:, o_ba:o_ba + d_conv].astype(F32) * conv

    zb = proj_ref[:, o_zb:o_zb + d_pool].astype(F32)
    zb_scr[0:HALO, :] = halo_ref[:, o_zb:o_zb + d_pool].astype(F32) * keep
    zb_scr[HALO:HALO + t_rows, :] = zb
    t_pos = seq_tile * t_rows + lax.broadcasted_iota(jnp.int32, (t_rows, 1), 0) + 1
    gdim = d_pool // len(POOL_WINDOWS)
    yb_parts = []
    for g, win in enumerate(POOL_WINDOWS):
        c0 = g * gdim
        s = zb[:, c0:c0 + gdim]
        for k in range(1, win):
            s = s + zb_scr[HALO - k:HALO - k + t_rows, c0:c0 + gdim]
        count = jnp.minimum(t_pos, win).astype(F32)
        pooled = s / count - zb[:, c0:c0 + gdim]
        yb_parts.append(jnp.dot(pooled.astype(BF16), wpool_ref[g].astype(BF16), preferred_element_type=F32))
    y_b = jnp.concatenate(yb_parts, axis=-1) * pscale_ref[...]

    v = proj_ref[:, o_v:o_v + d_sgu].astype(F32)
    mu = jnp.mean(v, axis=-1, keepdims=True)
    vc = v - mu
    var = jnp.mean(vc * vc, axis=-1, keepdims=True)
    vn = (vc * lax.rsqrt(var + EPS) * lng_ref[...] + lnb_ref[...]).astype(BF16)
    n_heads = d_sgu // HEAD_DIM
    row = lax.broadcasted_iota(jnp.int32, (CHUNK, CHUNK), 0)
    col = lax.broadcasted_iota(jnp.int32, (CHUNK, CHUNK), 1)
    yc_rows = []
    for c in range(t_rows // CHUNK):
        heads = []
        for hd in range(n_heads):
            ws = jnp.where(row >= col, wsp_ref[hd], 0.0).astype(BF16)
            vch = vn[c * CHUNK:(c + 1) * CHUNK, hd * HEAD_DIM:(hd + 1) * HEAD_DIM]
            heads.append(jnp.dot(ws, vch, preferred_element_type=F32) + bsp_ref[:, hd:hd + 1])
        yc_rows.append(jnp.concatenate(heads, axis=-1))
    y_c = proj_ref[:, o_u:o_u + d_sgu].astype(F32) * jnp.concatenate(yc_rows, axis=0)

    o_ref[:, 0:d_conv] = (_rms(y_a) * gout_ref[:, 0:d_conv]).astype(o_ref.dtype)
    o_ref[:, d_conv:d_conv + d_pool] = (_rms(y_b) * gout_ref[:, d_conv:d_conv + d_pool]).astype(o_ref.dtype)
    o_ref[:, d_conv + d_pool:] = (_rms(y_c) * gout_ref[:, d_conv + d_pool:]).astype(o_ref.dtype)


def _mixers(proj, conv_w, w_pool, pool_scale, ln_g, ln_b, w_spatial, b_spatial_t, g_out, *, seq, t_rows):
    n, d_in = proj.shape
    d_conv = conv_w.shape[1]
    d_pool = pool_scale.shape[1]
    d_sgu = ln_g.shape[1]
    d_mix = g_out.shape[1]
    halo_blocks = t_rows // HALO
    whole = lambda a: pl.BlockSpec(a.shape, lambda i: (0,) * a.ndim)
    body = functools.partial(_mixer_body, tiles_per_seq=seq // t_rows, d_conv=d_conv, d_pool=d_pool, d_sgu=d_sgu)
    return pl.pallas_call(
        body,
        grid=(n // t_rows,),
        in_specs=[
            pl.BlockSpec((t_rows, d_in), lambda i: (i, 0)),
            pl.BlockSpec((HALO, d_in), lambda i: (jnp.maximum(i * halo_blocks - 1, 0), 0)),
            whole(conv_w), whole(w_pool), whole(pool_scale), whole(ln_g), whole(ln_b),
            whole(w_spatial), whole(b_spatial_t), whole(g_out),
        ],
        out_specs=pl.BlockSpec((t_rows, d_mix), lambda i: (i, 0)),
        out_shape=jax.ShapeDtypeStruct((n, d_mix), BF16),
        scratch_shapes=[pltpu.VMEM((HALO + t_rows, d_conv), F32), pltpu.VMEM((HALO + t_rows, d_pool), F32)],
        compiler_params=_params(("arbitrary",)),
        name="mixers",
    )(proj, proj, conv_w, w_pool, pool_scale, ln_g, ln_b, w_spatial, b_spatial_t, g_out)


def _outproj_router_body(y_ref, h_ref, w_ref, g_ref, wrh_ref, wrl_ref, br_ref, h1_ref, xn_ref, idx_ref, prob_ref):
    h1 = h_ref[...] + jnp.dot(y_ref[...], w_ref[...], preferred_element_type=F32)
    h1_ref[...] = h1
    xn = _rms(h1) * g_ref[...]
    xn_ref[...] = xn
    x_hi = xn.astype(BF16)
    x_lo = (xn - x_hi.astype(F32)).astype(BF16)
    logits = (jnp.dot(x_hi, wrh_ref[...], preferred_element_type=F32)
              + (jnp.dot(x_lo, wrh_ref[...], preferred_element_type=F32)
                 + jnp.dot(x_hi, wrl_ref[...], preferred_element_type=F32))) + br_ref[...]
    lane = lax.broadcasted_iota(jnp.int32, logits.shape, 1)
    vals, idxs = [], []
    for _ in range(TOP_K):
        m = jnp.max(logits, axis=-1, keepdims=True)
        sel = jnp.min(jnp.where(logits == m, lane, ROUTER_LANES), axis=-1, keepdims=True)
        vals.append(m)
        idxs.append(sel)
        logits = jnp.where(lane == sel, -jnp.inf, logits)
    exps = [jnp.exp(v - vals[0]) for v in vals]
    denom = exps[0]
    for e in exps[1:]:
        denom = denom + e
    idx_out = jnp.zeros(lane.shape, jnp.int32)
    prob_out = jnp.zeros(lane.shape, F32)
    for k in range(TOP_K):
        idx_out = jnp.where(lane == k, idxs[k], idx_out)
        prob_out = jnp.where(lane == k, exps[k] / denom, prob_out)
    idx_ref[...] = idx_out
    prob_ref[...] = prob_out


def _outproj_router(y, h, w_out_bf16, g_ffn, w_router_hi, w_router_lo, b_router_pad, *, tm):
    n, d = h.shape
    whole = lambda a: pl.BlockSpec(a.shape, lambda i: (0,) * a.ndim)
    row_block = lambda width: pl.BlockSpec((tm, width), lambda i: (i, 0))
    return pl.pallas_call(
        _outproj_router_body,
        grid=(n // tm,),
        in_specs=[row_block(y.shape[1]), row_block(d), whole(w_out_bf16), whole(g_ffn),
                  whole(w_router_hi), whole(w_router_lo), whole(b_router_pad)],
        out_specs=[row_block(d), row_block(d), row_block(ROUTER_LANES), row_block(ROUTER_LANES)],
        out_shape=[jax.ShapeDtypeStruct((n, d), F32), jax.ShapeDtypeStruct((n, d), F32),
                   jax.ShapeDtypeStruct((n, ROUTER_LANES), jnp.int32),
                   jax.ShapeDtypeStruct((n, ROUTER_LANES), F32)],
        compiler_params=_params(("arbitrary",)),
        name="outproj_router",
    )(y, h, w_out_bf16, g_ffn, w_router_hi, w_router_lo, b_router_pad)


def _routing_plan(top_idx, n_experts):
    n = top_idx.shape[0]
    tile_rows = SUB * TILE_SUBS
    max_subs = (n * TOP_K) // SUB + n_experts
    max_tiles = max_subs // TILE_SUBS + n_experts
    hit = (top_idx[:, :, None] == jnp.arange(n_experts, dtype=jnp.int32)).any(axis=1).astype(jnp.int32)
    before = jnp.cumsum(hit, axis=0) - hit
    counts = jnp.sum(hit, axis=0)
    rank = jnp.take_along_axis(before, top_idx, axis=1)
    nsub = (counts + SUB - 1) // SUB
    sub_end = jnp.cumsum(nsub)
    sub_off = sub_end - nsub
    pos = sub_off[top_idx] * SUB + rank
    token = jnp.broadcast_to(jnp.arange(n, dtype=jnp.int32)[:, None], pos.shape)
    src = jnp.zeros((max_subs * SUB,), jnp.int32).at[pos.reshape(-1)].set(token.reshape(-1))
    ntile = (nsub + TILE_SUBS - 1) // TILE_SUBS
    tile_end = jnp.cumsum(ntile)
    tile_off = tile_end - ntile
    n_tiles = tile_end[-1]
    t = jnp.arange(max_tiles, dtype=jnp.int32)
    t_eff = jnp.minimum(t, n_tiles - 1)
    expert = jnp.sum((tile_end[None, :] <= t_eff[:, None]).astype(jnp.int32), axis=1)
    within = t_eff - tile_off[expert]
    start_sub = sub_off[expert] + within * TILE_SUBS
    tile_nsub = jnp.where(t < n_tiles, jnp.minimum(TILE_SUBS, nsub[expert] - within * TILE_SUBS), 0)
    rows = jnp.minimum(start_sub[:, None] * SUB + jnp.arange(tile_rows, dtype=jnp.int32)[None, :],
                       max_subs * SUB - 1)
    tile_src = src[rows].reshape(max_tiles, 1, tile_rows)
    return dict(pos=pos, tile_src=tile_src, tile_expert=expert, tile_nsub=tile_nsub.astype(jnp.int32),
                used_subs=sub_end[-1:].astype(jnp.int32),
                tile_start=start_sub.astype(jnp.int32), n_rows=max_subs * SUB)


def _row_copy(src_hbm, dst_vmem, sem, src_row, dst_row):
    return pltpu.make_async_copy(src_hbm.at[pl.ds(src_row, 1)], dst_vmem.at[pl.ds(dst_row, 1)], sem)


def _moe_body(expert_ref, nsub_ref, start_ref, used_ref, src_ref, x_hbm, wg_ref, bg_ref, wu_ref, bu_ref, wd_ref,
              bd_ref, ys_hbm, xg, xb, acc, wgb, wub, wdb, gsem, osem):
    del expert_ref
    t = pl.program_id(0)
    f = pl.program_id(1)
    n_f = pl.num_programs(1)
    ns = nsub_ref[t]

    def sub_rows(j):
        return pl.ds(pl.multiple_of(j * SUB, SUB), SUB)

    @pl.when((t == 0) & (f == 0))
    def _():
        used = used_ref[0]
        n_tail = ys_hbm.shape[0] // SUB - used
        xg[0:SUB, :] = jnp.zeros((SUB, xg.shape[1]), F32)

        def tail_copy(j):
            dst = pl.ds(pl.multiple_of((used + j) * SUB, SUB), SUB)
            return pltpu.make_async_copy(xg.at[pl.ds(0, SUB)], ys_hbm.at[dst], osem)

        def push(j, carry):
            tail_copy(j).start()
            return carry
        lax.fori_loop(0, n_tail, push, 0)

        def drain(j, carry):
            tail_copy(j).wait()
            return carry
        lax.fori_loop(0, n_tail, drain, 0)

    @pl.when(ns > 0)
    def _():
        @pl.when(f == 0)
        def _():
            def issue(c, carry):
                for u in range(ISSUE_UNROLL):
                    r = c * ISSUE_UNROLL + u
                    _row_copy(x_hbm, xg, gsem, src_ref[0, r], r).start()
                return carry
            lax.fori_loop(0, ns * (SUB // ISSUE_UNROLL), issue, 0)

            def land(j, carry):
                pltpu.make_async_copy(x_hbm.at[pl.ds(0, SUB)], xg.at[sub_rows(j)], gsem).wait()
                return carry
            lax.fori_loop(0, ns, land, 0)

            def narrow(j, carry):
                xb[sub_rows(j), :] = xg[sub_rows(j), :].astype(BF16)
                acc[sub_rows(j), :] = jnp.broadcast_to(bd_ref[...], (SUB, acc.shape[1]))
                return carry
            lax.fori_loop(0, ns, narrow, 0)

        wgb[...] = wg_ref[...].astype(BF16)
        wub[...] = wu_ref[...].astype(BF16)
        wdb[...] = wd_ref[...].astype(BF16)

        def ffn(j, carry):
            x = xb[sub_rows(j), :]
            g = jnp.minimum(jnp.dot(x, wgb[...], preferred_element_type=F32) + bg_ref[...], SWIGLU_LIMIT)
            u = jnp.clip(jnp.dot(x, wub[...], preferred_element_type=F32) + bu_ref[...], -SWIGLU_LIMIT, SWIGLU_LIMIT)
            a = (u + 1.0) * g * jax.nn.sigmoid(SWIGLU_ALPHA * g)
            acc[sub_rows(j), :] += jnp.dot(a.astype(BF16), wdb[...], preferred_element_type=F32)
            return carry
        lax.fori_loop(0, ns, ffn, 0)

        @pl.when(f == n_f - 1)
        def _():
            def out_copy(j):
                dst = pl.ds(pl.multiple_of((start_ref[t] + j) * SUB, SUB), SUB)
                return pltpu.make_async_copy(acc.at[sub_rows(j)], ys_hbm.at[dst], osem)

            def push(j, carry):
                out_copy(j).start()
                return carry
            lax.fori_loop(0, ns, push, 0)

            def drain(j, carry):
                out_copy(j).wait()
                return carry
            lax.fori_loop(0, ns, drain, 0)


def _moe_ffn(xn, plan, layer, w_gate, b_gate, w_up, b_up, w_down, b_down):
    n, d = xn.shape
    depth, n_experts, _, d_ff = w_gate.shape
    tile_rows = SUB * TILE_SUBS
    max_tiles = plan["tile_expert"].shape[0]
    n_f = d_ff // FF_BLOCK

    def ff_block(f, nsub_ref, t):
        return jnp.where(nsub_ref[t] > 0, f, n_f - 1)

    col_w = pl.BlockSpec((None, None, d, FF_BLOCK),
                         lambda t, f, e, ns, st, used: (layer, e[t], 0, ff_block(f, ns, t)))
    col_b = pl.BlockSpec((None, None, 1, FF_BLOCK),
                         lambda t, f, e, ns, st, used: (layer, e[t], 0, ff_block(f, ns, t)))
    grid_spec = pltpu.PrefetchScalarGridSpec(
        num_scalar_prefetch=4,
        grid=(max_tiles, n_f),
        in_specs=[
            pl.BlockSpec((None, 1, tile_rows), lambda t, f, e, ns, st, used: (t, 0, 0), memory_space=pltpu.SMEM),
            pl.BlockSpec(memory_space=pl.ANY),
            col_w, col_b, col_w, col_b,
            pl.BlockSpec((None, None, FF_BLOCK, d),
                         lambda t, f, e, ns, st, used: (layer, e[t], ff_block(f, ns, t), 0)),
            pl.BlockSpec((None, None, 1, d), lambda t, f, e, ns, st, used: (layer, e[t], 0, 0)),
        ],
        out_specs=pl.BlockSpec(memory_space=pl.ANY),
        scratch_shapes=[
            pltpu.VMEM((tile_rows, d), F32), pltpu.VMEM((tile_rows, d), BF16), pltpu.VMEM((tile_rows, d), F32),
            pltpu.VMEM((d, FF_BLOCK), BF16), pltpu.VMEM((d, FF_BLOCK), BF16), pltpu.VMEM((FF_BLOCK, d), BF16),
            pltpu.SemaphoreType.DMA, pltpu.SemaphoreType.DMA,
        ],
    )
    return pl.pallas_call(
        _moe_body,
        grid_spec=grid_spec,
        out_shape=jax.ShapeDtypeStruct((plan["n_rows"], d), F32),
        compiler_params=_params(("arbitrary", "arbitrary")),
        name="moe_ffn",
    )(plan["tile_expert"], plan["tile_nsub"], plan["tile_start"], plan["used_subs"], plan["tile_src"], xn,
      w_gate, b_gate.reshape(depth, n_experts, 1, d_ff), w_up, b_up.reshape(depth, n_experts, 1, d_ff),
      w_down, b_down.reshape(depth, n_experts, 1, d))


def _ple_body(pos_ref, prob_ref, h_ref, ys_hbm, p_ref, gin_ref, wg_ref, wp_ref, gpost_ref, gfin_ref, o_ref,
              yg, sem, *, final):
    tp = h_ref.shape[0]

    def issue(r, carry):
        for k in range(TOP_K):
            _row_copy(ys_hbm, yg.at[k], sem, pos_ref[0, k * tp + r], r).start()
        return carry
    lax.fori_loop(0, tp, issue, 0, unroll=2)
    for k in range(TOP_K):
        pltpu.make_async_copy(ys_hbm.at[pl.ds(0, tp)], yg.at[k], sem).wait()

    h2 = h_ref[...]
    for k in range(TOP_K):
        h2 = h2 + prob_ref[:, k:k + 1] * yg[k]
    xn = (_rms(h2) * gin_ref[...]).astype(BF16)
    gate = jax.nn.sigmoid(jnp.dot(xn, wg_ref[...], preferred_element_type=F32))
    ple = _rms(jnp.dot(p_ref[...].astype(BF16), wp_ref[...], preferred_element_type=F32)) * gpost_ref[...]
    h3 = h2 + gate * ple
    if final:
        h3 = _rms(h3) * gfin_ref[...]
    o_ref[...] = h3


def _combine_ple(pos, probs, h1, ys, p, g_in, w_gate_bf16, w_proj_bf16, g_post, g_final, *, tp, final):
    n, d = h1.shape
    d_ple = p.shape[1]
    pos_tiles = pos.reshape(n // tp, tp, TOP_K).transpose(0, 2, 1).reshape(n // tp, 1, TOP_K * tp)
    whole = lambda a: pl.BlockSpec(a.shape, lambda i: (0,) * a.ndim)
    row_block = lambda width: pl.BlockSpec((tp, width), lambda i: (i, 0))
    return pl.pallas_call(
        functools.partial(_ple_body, final=final),
        grid=(n // tp,),
        in_specs=[
            pl.BlockSpec((None, 1, TOP_K * tp), lambda i: (i, 0, 0), memory_space=pltpu.SMEM),
            row_block(TOP_K), row_block(d), pl.BlockSpec(memory_space=pl.ANY), row_block(d_ple),
            whole(g_in), whole(w_gate_bf16), whole(w_proj_bf16), whole(g_post), whole(g_final),
        ],
        out_specs=row_block(d),
        out_shape=jax.ShapeDtypeStruct((n, d), F32),
        scratch_shapes=[pltpu.VMEM((TOP_K, tp, d), F32), pltpu.SemaphoreType.DMA],
        compiler_params=_params(("arbitrary",)),
        name="combine_ple",
    )(pos_tiles, probs, h1, ys, p, g_in, w_gate_bf16, w_proj_bf16, g_post, g_final)


def kernel(x, p, g_mix_norm, w_in, conv_w, w_pool, pool_scale, ln_v_g, ln_v_b, w_spatial, b_spatial, g_out,
           w_out, g_ffn_norm, w_router, b_router, w_gate, b_gate, w_up, b_up, w_down, b_down, g_ple_in,
           w_ple_gate, w_ple_proj, g_ple_post, g_final):
    batch, seq, d = x.shape
    depth = w_in.shape[0]
    n = batch * seq
    n_experts = w_router.shape[2]
    row = lambda v: v.reshape(1, -1)

    h = x.reshape(n, d)
    for i in range(depth):
        proj = _inproj(h, row(g_mix_norm[i]), w_in[i].astype(BF16), tm=512, tn=w_in.shape[2] // 2)
        y = _mixers(proj, conv_w[i], w_pool[i], row(pool_scale[i]), row(ln_v_g[i]), row(ln_v_b[i]),
                    w_spatial[i], b_spatial[i].T, row(g_out[i]), seq=seq, t_rows=256)
        w_router_pad = jnp.pad(w_router[i], ((0, 0), (0, ROUTER_LANES - n_experts)))
        w_router_hi = w_router_pad.astype(BF16)
        w_router_lo = (w_router_pad - w_router_hi.astype(F32)).astype(BF16)
        b_router_pad = jnp.pad(row(b_router[i]), ((0, 0), (0, ROUTER_LANES - n_experts)), constant_values=-jnp.inf)
        h1, xn, idx_pad, prob_pad = _outproj_router(y, h, w_out[i].astype(BF16), row(g_ffn_norm[i]),
                                                    w_router_hi, w_router_lo, b_router_pad, tm=256)
        top_idx, probs = idx_pad[:, :TOP_K], prob_pad[:, :TOP_K]
        plan = _routing_plan(top_idx, n_experts)
        ys = _moe_ffn(xn, plan, i, w_gate, b_gate, w_up, b_up, w_down, b_down)
        h = _combine_ple(plan["pos"], probs, h1, ys, p[i].reshape(n, -1), row(g_ple_in[i]),
                         w_ple_gate[i].astype(BF16), w_ple_proj[i].astype(BF16), row(g_ple_post[i]),
                         row(g_final), tp=256, final=(i == depth - 1))
    return h.reshape(batch, seq, d)
```

```python
import functools

import jax
import jax.numpy as jnp
from jax import lax
from jax.experimental import pallas as pl
from jax.experimental.pallas import tpu as pltpu

F32 = jnp.float32
BF16 = jnp.bfloat16

EPS = 1e-6
HEAD_DIM = 128
CHUNK = 128
CONV_WIDTH = 3
POOL_WINDOWS = (2, 4, 8, 16)
TOP_K = 4
SWIGLU_LIMIT = 7.0
SWIGLU_ALPHA = 1.702

HALO = 16
ROUTER_LANES = 128
VMEM_LIMIT = 56 * 1024 * 1024

SUB = 256
TILE_SUBS = 5
FF_BLOCK = 256
ISSUE_UNROLL = 8


def _rms(x):
    return x * lax.rsqrt(jnp.mean(x * x, axis=-1, keepdims=True) + EPS)


def _params(semantics):
    return pltpu.CompilerParams(dimension_semantics=semantics, vmem_limit_bytes=VMEM_LIMIT)


def _inproj_body(h_ref, g_ref, w_ref, o_ref):
    xn = _rms(h_ref[...]) * g_ref[...]
    o_ref[...] = jnp.dot(xn.astype(BF16), w_ref[...], preferred_element_type=F32).astype(o_ref.dtype)


def _inproj(h, g, w_bf16, *, tm, tn):
    n, d = h.shape
    d_in = w_bf16.shape[1]
    return pl.pallas_call(
        _inproj_body,
        grid=(d_in // tn, n // tm),
        in_specs=[
            pl.BlockSpec((tm, d), lambda j, i: (i, 0)),
            pl.BlockSpec((1, d), lambda j, i: (0, 0)),
            pl.BlockSpec((d, tn), lambda j, i: (0, j)),
        ],
        out_specs=pl.BlockSpec((tm, tn), lambda j, i: (i, j)),
        out_shape=jax.ShapeDtypeStruct((n, d_in), BF16),
        compiler_params=_params(("arbitrary", "arbitrary")),
        name="inproj",
    )(h, g, w_bf16)


def _mixer_body(proj_ref, halo_ref, convw_ref, wpool_ref, pscale_ref, lng_ref, lnb_ref,
                wsp_ref, bsp_ref, gout_ref, o_ref, za_scr, zb_scr, *, tiles_per_seq, d_conv, d_pool, d_sgu):
    t_rows = proj_ref.shape[0]
    i = pl.program_id(0)
    seq_tile = i % tiles_per_seq
    keep = (seq_tile > 0).astype(F32)

    o_za, o_ba, o_ca = 0, d_conv, 2 * d_conv
    o_zb = 3 * d_conv
    o_u = o_zb + d_pool
    o_v = o_u + d_sgu

    zc = proj_ref[:, o_ca:o_ca + d_conv].astype(F32) * proj_ref[:, o_za:o_za + d_conv].astype(F32)
    zc_halo = (halo_ref[:, o_ca:o_ca + d_conv].astype(F32) * halo_ref[:, o_za:o_za + d_conv].astype(F32)) * keep
    za_scr[0:HALO, :] = zc_halo
    za_scr[HALO:HALO + t_rows, :] = zc
    conv = zc * convw_ref[CONV_WIDTH - 1:CONV_WIDTH, :]
    for k in range(1, CONV_WIDTH):
        conv = conv + za_scr[HALO - k:HALO - k + t_rows, :] * convw_ref[CONV_WIDTH - 1 - k:CONV_WIDTH - k, :]
    y_a = proj_ref[:, o_ba:o_ba + d_conv].astype(F32) * conv

    zb = proj_ref[:, o_zb:o_zb + d_pool].astype(F32)
    zb_scr[0:HALO, :] = halo_ref[:, o_zb:o_zb + d_pool].astype(F32) * keep
    zb_scr[HALO:HALO + t_rows, :] = zb
    t_pos = seq_tile * t_rows + lax.broadcasted_iota(jnp.int32, (t_rows, 1), 0) + 1
    gdim = d_pool // len(POOL_WINDOWS)
    yb_parts = []
    for g, win in enumerate(POOL_WINDOWS):
        c0 = g * gdim
        s = zb[:, c0:c0 + gdim]
        for k in range(1, win):
            s = s + zb_scr[HALO - k:HALO - k + t_rows, c0:c0 + gdim]
        count = jnp.minimum(t_pos, win).astype(F32)
        pooled = s / count - zb[:, c0:c0 + gdim]
        yb_parts.append(jnp.dot(pooled.astype(BF16), wpool_ref[g].astype(BF16), preferred_element_type=F32))
    y_b = jnp.concatenate(yb_parts, axis=-1) * pscale_ref[...]

    v = proj_ref[:, o_v:o_v + d_sgu].astype(F32)
    mu = jnp.mean(v, axis=-1, keepdims=True)
    vc = v - mu
    var = jnp.mean(vc * vc, axis=-1, keepdims=True)
    vn = (vc * lax.rsqrt(var + EPS) * lng_ref[...] + lnb_ref[...]).astype(BF16)
    n_heads = d_sgu // HEAD_DIM
    row = lax.broadcasted_iota(jnp.int32, (CHUNK, CHUNK), 0)
    col = lax.broadcasted_iota(jnp.int32, (CHUNK, CHUNK), 1)
    yc_rows = []
    for c in range(t_rows // CHUNK):
        heads = []
        for hd in range(n_heads):
            ws = jnp.where(row >= col, wsp_ref[hd], 0.0).astype(BF16)
            vch = vn[c * CHUNK:(c + 1) * CHUNK, hd * HEAD_DIM:(hd + 1) * HEAD_DIM]
            heads.append(jnp.dot(ws, vch, preferred_element_type=F32) + bsp_ref[:, hd:hd + 1])
        yc_rows.append(jnp.concatenate(heads, axis=-1))
    y_c = proj_ref[:, o_u:o_u + d_sgu].astype(F32) * jnp.concatenate(yc_rows, axis=0)

    o_ref[:, 0:d_conv] = (_rms(y_a) * gout_ref[:, 0:d_conv]).astype(o_ref.dtype)
    o_ref[:, d_conv:d_conv + d_pool] = (_rms(y_b) * gout_ref[:, d_conv:d_conv + d_pool]).astype(o_ref.dtype)
    o_ref[:, d_conv + d_pool:] = (_rms(y_c) * gout_ref[:, d_conv + d_pool:]).astype(o_ref.dtype)


def _mixers(proj, conv_w, w_pool, pool_scale, ln_g, ln_b, w_spatial, b_spatial_t, g_out, *, seq, t_rows):
    n, d_in = proj.shape
    d_conv = conv_w.shape[1]
    d_pool = pool_scale.shape[1]
    d_sgu = ln_g.shape[1]
    d_mix = g_out.shape[1]
    halo_blocks = t_rows // HALO
    whole = lambda a: pl.BlockSpec(a.shape, lambda i: (0,) * a.ndim)
    body = functools.partial(_mixer_body, tiles_per_seq=seq // t_rows, d_conv=d_conv, d_pool=d_pool, d_sgu=d_sgu)
    return pl.pallas_call(
        body,
        grid=(n // t_rows,),
        in_specs=[
            pl.BlockSpec((t_rows, d_in), lambda i: (i, 0)),
            pl.BlockSpec((HALO, d_in), lambda i: (jnp.maximum(i * halo_blocks - 1, 0), 0)),
            whole(conv_w), whole(w_pool), whole(pool_scale), whole(ln_g), whole(ln_b),
            whole(w_spatial), whole(b_spatial_t), whole(g_out),
        ],
        out_specs=pl.BlockSpec((t_rows, d_mix), lambda i: (i, 0)),
        out_shape=jax.ShapeDtypeStruct((n, d_mix), BF16),
        scratch_shapes=[pltpu.VMEM((HALO + t_rows, d_conv), F32), pltpu.VMEM((HALO + t_rows, d_pool), F32)],
        compiler_params=_params(("arbitrary",)),
        name="mixers",
    )(proj, proj, conv_w, w_pool, pool_scale, ln_g, ln_b, w_spatial, b_spatial_t, g_out)


def _outproj_router_body(y_ref, h_ref, w_ref, g_ref, wrh_ref, wrl_ref, br_ref, h1_ref, xn_ref, idx_ref, prob_ref):
    h1 = h_ref[...] + jnp.dot(y_ref[...], w_ref[...], preferred_element_type=F32)
    h1_ref[...] = h1
    xn = _rms(h1) * g_ref[...]
    xn_ref[...] = xn
    x_hi = xn.astype(BF16)
    x_lo = (xn - x_hi.astype(F32)).astype(BF16)
    logits = (jnp.dot(x_hi, wrh_ref[...], preferred_element_type=F32)
              + (jnp.dot(x_lo, wrh_ref[...], preferred_element_type=F32)
                 + jnp.dot(x_hi, wrl_ref[...], preferred_element_type=F32))) + br_ref[...]
    lane = lax.broadcasted_iota(jnp.int32, logits.shape, 1)
    vals, idxs = [], []
    for _ in range(TOP_K):
        m = jnp.max(logits, axis=-1, keepdims=True)
        sel = jnp.min(jnp.where(logits == m, lane, ROUTER_LANES), axis=-1, keepdims=True)
        vals.append(m)
        idxs.append(sel)
        logits = jnp.where(lane == sel, -jnp.inf, logits)
    exps = [jnp.exp(v - vals[0]) for v in vals]
    denom = exps[0]
    for e in exps[1:]:
        denom = denom + e
    idx_out = jnp.zeros(lane.shape, jnp.int32)
    prob_out = jnp.zeros(lane.shape, F32)
    for k in range(TOP_K):
        idx_out = jnp.where(lane == k, idxs[k], idx_out)
        prob_out = jnp.where(lane == k, exps[k] / denom, prob_out)
    idx_ref[...] = idx_out
    prob_ref[...] = prob_out


def _outproj_router(y, h, w_out_bf16, g_ffn, w_router_hi, w_router_lo, b_router_pad, *, tm):
    n, d = h.shape
    whole = lambda a: pl.BlockSpec(a.shape, lambda i: (0,) * a.ndim)
    row_block = lambda width: pl.BlockSpec((tm, width), lambda i: (i, 0))
    return pl.pallas_call(
        _outproj_router_body,
        grid=(n // tm,),
        in_specs=[row_block(y.shape[1]), row_block(d), whole(w_out_bf16), whole(g_ffn),
                  whole(w_router_hi), whole(w_router_lo), whole(b_router_pad)],
        out_specs=[row_block(d), row_block(d), row_block(ROUTER_LANES), row_block(ROUTER_LANES)],
        out_shape=[jax.ShapeDtypeStruct((n, d), F32), jax.ShapeDtypeStruct((n, d), F32),
                   jax.ShapeDtypeStruct((n, ROUTER_LANES), jnp.int32),
                   jax.ShapeDtypeStruct((n, ROUTER_LANES), F32)],
        compiler_params=_params(("arbitrary",)),
        name="outproj_router",
    )(y, h, w_out_bf16, g_ffn, w_router_hi, w_router_lo, b_router_pad)


def _routing_plan(top_idx, n_experts):
    n = top_idx.shape[0]
    max_subs = (n * TOP_K) // SUB + n_experts
    max_tiles = max_subs // TILE_SUBS + n_experts
    hit = (top_idx[:, :, None] == jnp.arange(n_experts, dtype=jnp.int32)).any(axis=1).astype(jnp.int32)
    before = jnp.cumsum(hit, axis=0) - hit
    counts = jnp.sum(hit, axis=0)
    rank = jnp.take_along_axis(before, top_idx, axis=1)
    nsub = (counts + SUB - 1) // SUB
    sub_end = jnp.cumsum(nsub)
    sub_off = sub_end - nsub
    pos = sub_off[top_idx] * SUB + rank
    block = jnp.arange(max_subs, dtype=jnp.int32)
    is_last = ((block[:, None] == (sub_end - 1)[None, :]) & (nsub > 0)[None, :]).any(axis=1)
    pad_block = (is_last | (block >= sub_end[-1])).astype(jnp.int32)
    ntile = (nsub + TILE_SUBS - 1) // TILE_SUBS
    tile_end = jnp.cumsum(ntile)
    tile_off = tile_end - ntile
    n_tiles = tile_end[-1]
    t = jnp.arange(max_tiles, dtype=jnp.int32)
    t_eff = jnp.minimum(t, n_tiles - 1)
    expert = jnp.sum((tile_end[None, :] <= t_eff[:, None]).astype(jnp.int32), axis=1)
    within = t_eff - tile_off[expert]
    start_sub = sub_off[expert] + within * TILE_SUBS
    tile_nsub = jnp.where(t < n_tiles, jnp.minimum(TILE_SUBS, nsub[expert] - within * TILE_SUBS), 0)
    return dict(pos=pos, tile_expert=expert, tile_nsub=tile_nsub.astype(jnp.int32),
                tile_start=start_sub.astype(jnp.int32), used_subs=sub_end[-1:].astype(jnp.int32),
                pad_block=pad_block,
                n_pad_blocks=jnp.sum(pad_block).reshape(1), n_rows=max_subs * SUB)


def _pos_tiles(pos, tp):
    n = pos.shape[0]
    return pos.reshape(n // tp, tp, TOP_K).transpose(0, 2, 1).reshape(n // tp, 1, TOP_K * tp)


def _dispatch_body(pad_ref, npad_ref, pos_ref, x_ref, xs_hbm, zeros, sem):
    i = pl.program_id(0)
    tp, d = x_ref.shape

    def block_copy(b):
        return pltpu.make_async_copy(zeros, xs_hbm.at[pl.ds(pl.multiple_of(b * SUB, SUB), SUB)], sem)

    @pl.when(i == 0)
    def _():
        zeros[...] = jnp.zeros(zeros.shape, zeros.dtype)

        def push(b, carry):
            @pl.when(pad_ref[b] > 0)
            def _():
                block_copy(b).start()
            return carry
        lax.fori_loop(0, pad_ref.shape[0], push, 0)

        def drain(b, carry):
            block_copy(0).wait()
            return carry
        lax.fori_loop(0, npad_ref[0], drain, 0)

    def issue(c, carry):
        for u in range(ISSUE_UNROLL):
            r = c * ISSUE_UNROLL + u
            for k in range(TOP_K):
                pltpu.make_async_copy(x_ref.at[pl.ds(r, 1)], xs_hbm.at[pl.ds(pos_ref[0, k * tp + r], 1)], sem).start()
        return carry
    lax.fori_loop(0, tp // ISSUE_UNROLL, issue, 0)
    for k in range(TOP_K):
        pltpu.make_async_copy(x_ref, xs_hbm.at[pl.ds(0, tp)], sem).wait()


def _dispatch(xn, plan, *, tp):
    n, d = xn.shape
    grid_spec = pltpu.PrefetchScalarGridSpec(
        num_scalar_prefetch=2,
        grid=(n // tp,),
        in_specs=[
            pl.BlockSpec((None, 1, TOP_K * tp), lambda i, pad, npad: (i, 0, 0), memory_space=pltpu.SMEM),
            pl.BlockSpec((tp, d), lambda i, pad, npad: (i, 0)),
        ],
        out_specs=pl.BlockSpec(memory_space=pl.ANY),
        scratch_shapes=[pltpu.VMEM((SUB, d), xn.dtype), pltpu.SemaphoreType.DMA],
    )
    return pl.pallas_call(
        _dispatch_body,
        grid_spec=grid_spec,
        out_shape=jax.ShapeDtypeStruct((plan["n_rows"], d), xn.dtype),
        compiler_params=_params(("arbitrary",)),
        name="dispatch",
    )(plan["pad_block"], plan["n_pad_blocks"], _pos_tiles(plan["pos"], tp), xn)


def _row_copy(src_hbm, dst_vmem, sem, src_row, dst_row):
    return pltpu.make_async_copy(src_hbm.at[pl.ds(src_row, 1)], dst_vmem.at[pl.ds(dst_row, 1)], sem)


def _moe_body(expert_ref, nsub_ref, start_ref, used_ref, xs_hbm, wg_ref, bg_ref, wu_ref, bu_ref, wd_ref, bd_ref,
              ys_hbm, xg, xb, acc, wgb, wub, wdb, gsem, osem, *, n_t, n_f):
    del expert_ref
    t = pl.program_id(0)
    f = pl.program_id(1)
    ns = nsub_ref[t]
    t_next = jnp.minimum(t + 1, n_t - 1)
    ns_next = jnp.where(t + 1 < n_t, nsub_ref[t_next], 0)

    def sub_rows(j):
        return pl.ds(pl.multiple_of(j * SUB, SUB), SUB)

    def in_copy(tile, j):
        src = pl.ds(pl.multiple_of((start_ref[tile] + j) * SUB, SUB), SUB)
        return pltpu.make_async_copy(xs_hbm.at[src], xg.at[sub_rows(j)], gsem)

    def out_copy(tile, j):
        dst = pl.ds(pl.multiple_of((start_ref[tile] + j) * SUB, SUB), SUB)
        return pltpu.make_async_copy(acc.at[sub_rows(j)], ys_hbm.at[dst], osem)

    @pl.when((t == 0) & (f == 0))
    def _():
        used = used_ref[0]
        n_tail = ys_hbm.shape[0] // SUB - used
        xg[0:SUB, :] = jnp.zeros((SUB, xg.shape[1]), F32)

        def tail_copy(j):
            dst = pl.ds(pl.multiple_of((used + j) * SUB, SUB), SUB)
            return pltpu.make_async_copy(xg.at[pl.ds(0, SUB)], ys_hbm.at[dst], osem)

        def push(j, carry):
            tail_copy(j).start()
            return carry
        lax.fori_loop(0, n_tail, push, 0)

        def drain(j, carry):
            tail_copy(j).wait()
            return carry
        lax.fori_loop(0, n_tail, drain, 0)

    @pl.when((f == 0) & (t > 0))
    def _():
        def drain(j, carry):
            out_copy(t - 1, j).wait()
            return carry
        lax.fori_loop(0, nsub_ref[jnp.maximum(t - 1, 0)], drain, 0)

    @pl.when(ns > 0)
    def _():
        @pl.when(f == 0)
        def _():
            @pl.when(t == 0)
            def _():
                def fetch(j, carry):
                    in_copy(t, j).start()
                    return carry
                lax.fori_loop(0, ns, fetch, 0)

            def land(j, carry):
                in_copy(t, j).wait()
                return carry
            lax.fori_loop(0, ns, land, 0)

            def narrow(j, carry):
                xb[sub_rows(j), :] = xg[sub_rows(j), :].astype(BF16)
                acc[sub_rows(j), :] = jnp.broadcast_to(bd_ref[...], (SUB, acc.shape[1]))
                return carry
            lax.fori_loop(0, ns, narrow, 0)

            def prefetch(j, carry):
                in_copy(t_next, j).start()
                return carry
            lax.fori_loop(0, ns_next, prefetch, 0)

        wgb[...] = wg_ref[...].astype(BF16)
        wub[...] = wu_ref[...].astype(BF16)
        wdb[...] = wd_ref[...].astype(BF16)

        def ffn(j, carry):
            x = xb[sub_rows(j), :]
            g = jnp.minimum(jnp.dot(x, wgb[...], preferred_element_type=F32) + bg_ref[...], SWIGLU_LIMIT)
            u = jnp.clip(jnp.dot(x, wub[...], preferred_element_type=F32) + bu_ref[...], -SWIGLU_LIMIT, SWIGLU_LIMIT)
            a = (u + 1.0) * g * jax.nn.sigmoid(SWIGLU_ALPHA * g)
            acc[sub_rows(j), :] += jnp.dot(a.astype(BF16), wdb[...], preferred_element_type=F32)
            return carry
        lax.fori_loop(0, ns, ffn, 0)

        @pl.when(f == n_f - 1)
        def _():
            def push(j, carry):
                out_copy(t, j).start()
                return carry
            lax.fori_loop(0, ns, push, 0)

            @pl.when(t == n_t - 1)
            def _():
                def drain(j, carry):
                    out_copy(t, j).wait()
                    return carry
                lax.fori_loop(0, ns, drain, 0)


def _moe_ffn(xs, plan, layer, w_gate, b_gate, w_up, b_up, w_down, b_down):
    d = xs.shape[1]
    depth, n_experts, _, d_ff = w_gate.shape
    tile_rows = SUB * TILE_SUBS
    max_tiles = plan["tile_expert"].shape[0]
    n_f = d_ff // FF_BLOCK

    def ff_block(f, nsub_ref, t):
        return jnp.where(nsub_ref[t] > 0, f, n_f - 1)

    col_w = pl.BlockSpec((None, None, d, FF_BLOCK),
                         lambda t, f, e, ns, st, used: (layer, e[t], 0, ff_block(f, ns, t)))
    col_b = pl.BlockSpec((None, None, 1, FF_BLOCK),
                         lambda t, f, e, ns, st, used: (layer, e[t], 0, ff_block(f, ns, t)))
    grid_spec = pltpu.PrefetchScalarGridSpec(
        num_scalar_prefetch=4,
        grid=(max_tiles, n_f),
        in_specs=[
            pl.BlockSpec(memory_space=pl.ANY),
            col_w, col_b, col_w, col_b,
            pl.BlockSpec((None, None, FF_BLOCK, d),
                         lambda t, f, e, ns, st, used: (layer, e[t], ff_block(f, ns, t), 0)),
            pl.BlockSpec((None, None, 1, d), lambda t, f, e, ns, st, used: (layer, e[t], 0, 0)),
        ],
        out_specs=pl.BlockSpec(memory_space=pl.ANY),
        scratch_shapes=[
            pltpu.VMEM((tile_rows, d), F32), pltpu.VMEM((tile_rows, d), BF16), pltpu.VMEM((tile_rows, d), F32),
            pltpu.VMEM((d, FF_BLOCK), BF16), pltpu.VMEM((d, FF_BLOCK), BF16), pltpu.VMEM((FF_BLOCK, d), BF16),
            pltpu.SemaphoreType.DMA, pltpu.SemaphoreType.DMA,
        ],
    )
    return pl.pallas_call(
        functools.partial(_moe_body, n_t=max_tiles, n_f=n_f),
        grid_spec=grid_spec,
        out_shape=jax.ShapeDtypeStruct((plan["n_rows"], d), F32),
        compiler_params=_params(("arbitrary", "arbitrary")),
        name="moe_ffn",
    )(plan["tile_expert"], plan["tile_nsub"], plan["tile_start"], plan["used_subs"], xs,
      w_gate, b_gate.reshape(depth, n_experts, 1, d_ff), w_up, b_up.reshape(depth, n_experts, 1, d_ff),
      w_down, b_down.reshape(depth, n_experts, 1, d))


def _ple_body(pos_ref, prob_ref, h_ref, ys_hbm, p_ref, gin_ref, wg_ref, wp_ref, gpost_ref, gfin_ref, o_ref,
              yg, sem, *, final):
    tp = h_ref.shape[0]

    def issue(r, carry):
        for k in range(TOP_K):
            _row_copy(ys_hbm, yg.at[k], sem, pos_ref[0, k * tp + r], r).start()
        return carry
    lax.fori_loop(0, tp, issue, 0, unroll=2)
    for k in range(TOP_K):
        pltpu.make_async_copy(ys_hbm.at[pl.ds(0, tp)], yg.at[k], sem).wait()

    h2 = h_ref[...]
    for k in range(TOP_K):
        h2 = h2 + prob_ref[:, k:k + 1] * yg[k]
    xn = (_rms(h2) * gin_ref[...]).astype(BF16)
    gate = jax.nn.sigmoid(jnp.dot(xn, wg_ref[...], preferred_element_type=F32))
    ple = _rms(jnp.dot(p_ref[...].astype(BF16), wp_ref[...], preferred_element_type=F32)) * gpost_ref[...]
    h3 = h2 + gate * ple
    if final:
        h3 = _rms(h3) * gfin_ref[...]
    o_ref[...] = h3


def _combine_ple(pos, probs, h1, ys, p, g_in, w_gate_bf16, w_proj_bf16, g_post, g_final, *, tp, final):
    n, d = h1.shape
    d_ple = p.shape[1]
    pos_tiles = _pos_tiles(pos, tp)
    whole = lambda a: pl.BlockSpec(a.shape, lambda i: (0,) * a.ndim)
    row_block = lambda width: pl.BlockSpec((tp, width), lambda i: (i, 0))
    return pl.pallas_call(
        functools.partial(_ple_body, final=final),
        grid=(n // tp,),
        in_specs=[
            pl.BlockSpec((None, 1, TOP_K * tp), lambda i: (i, 0, 0), memory_space=pltpu.SMEM),
            row_block(TOP_K), row_block(d), pl.BlockSpec(memory_space=pl.ANY), row_block(d_ple),
            whole(g_in), whole(w_gate_bf16), whole(w_proj_bf16), whole(g_post), whole(g_final),
        ],
        out_specs=row_block(d),
        out_shape=jax.ShapeDtypeStruct((n, d), F32),
        scratch_shapes=[pltpu.VMEM((TOP_K, tp, d), F32), pltpu.SemaphoreType.DMA],
        compiler_params=_params(("arbitrary",)),
        name="combine_ple",
    )(pos_tiles, probs, h1, ys, p, g_in, w_gate_bf16, w_proj_bf16, g_post, g_final)


def kernel(x, p, g_mix_norm, w_in, conv_w, w_pool, pool_scale, ln_v_g, ln_v_b, w_spatial, b_spatial, g_out,
           w_out, g_ffn_norm, w_router, b_router, w_gate, b_gate, w_up, b_up, w_down, b_down, g_ple_in,
           w_ple_gate, w_ple_proj, g_ple_post, g_final):
    batch, seq, d = x.shape
    depth = w_in.shape[0]
    n = batch * seq
    n_experts = w_router.shape[2]
    row = lambda v: v.reshape(1, -1)

    h = x.reshape(n, d)
    for i in range(depth):
        proj = _inproj(h, row(g_mix_norm[i]), w_in[i].astype(BF16), tm=512, tn=w_in.shape[2] // 2)
        y = _mixers(proj, conv_w[i], w_pool[i], row(pool_scale[i]), row(ln_v_g[i]), row(ln_v_b[i]),
                    w_spatial[i], b_spatial[i].T, row(g_out[i]), seq=seq, t_rows=256)
        w_router_pad = jnp.pad(w_router[i], ((0, 0), (0, ROUTER_LANES - n_experts)))
        w_router_hi = w_router_pad.astype(BF16)
        w_router_lo = (w_router_pad - w_router_hi.astype(F32)).astype(BF16)
        b_router_pad = jnp.pad(row(b_router[i]), ((0, 0), (0, ROUTER_LANES - n_experts)), constant_values=-jnp.inf)
        h1, xn, idx_pad, prob_pad = _outproj_router(y, h, w_out[i].astype(BF16), row(g_ffn_norm[i]),
                                                    w_router_hi, w_router_lo, b_router_pad, tm=256)
        top_idx, probs = idx_pad[:, :TOP_K], prob_pad[:, :TOP_K]
        plan = _routing_plan(top_idx, n_experts)
        xs = _dispatch(xn, plan, tp=256)
        ys = _moe_ffn(xs, plan, i, w_gate, b_gate, w_up, b_up, w_down, b_down)
        h = _combine_ple(plan["pos"], probs, h1, ys, p[i].reshape(n, -1), row(g_ple_in[i]),
                         w_ple_gate[i].astype(BF16), w_ple_proj[i].astype(BF16), row(g_ple_post[i]),
                         row(g_final), tp=256, final=(i == depth - 1))
    return h.reshape(batch, seq, d)
```

```python
import functools

import jax
import jax.numpy as jnp
from jax import lax
from jax.experimental import pallas as pl
from jax.experimental.pallas import tpu as pltpu

F32 = jnp.float32
BF16 = jnp.bfloat16

EPS = 1e-6
HEAD_DIM = 128
CHUNK = 128
CONV_WIDTH = 3
POOL_WINDOWS = (2, 4, 8, 16)
TOP_K = 4
SWIGLU_LIMIT = 7.0
SWIGLU_ALPHA = 1.702

HALO = 16
ROUTER_LANES = 128
VMEM_LIMIT = 56 * 1024 * 1024

SUB = 256
TILE_SUBS = 5
FF_BLOCK = 256
ISSUE_UNROLL = 8


def _rms(x):
    return x * lax.rsqrt(jnp.mean(x * x, axis=-1, keepdims=True) + EPS)


def _params(semantics):
    return pltpu.CompilerParams(dimension_semantics=semantics, vmem_limit_bytes=VMEM_LIMIT)


def _inproj_body(h_ref, g_ref, w_ref, o_ref):
    xn = _rms(h_ref[...]) * g_ref[...]
    o_ref[...] = jnp.dot(xn.astype(BF16), w_ref[...], preferred_element_type=F32).astype(o_ref.dtype)


def _inproj(h, g, w_bf16, *, tm, tn):
    n, d = h.shape
    d_in = w_bf16.shape[1]
    return pl.pallas_call(
        _inproj_body,
        grid=(d_in // tn, n // tm),
        in_specs=[
            pl.BlockSpec((tm, d), lambda j, i: (i, 0)),
            pl.BlockSpec((1, d), lambda j, i: (0, 0)),
            pl.BlockSpec((d, tn), lambda j, i: (0, j)),
        ],
        out_specs=pl.BlockSpec((tm, tn), lambda j, i: (i, j)),
        out_shape=jax.ShapeDtypeStruct((n, d_in), BF16),
        compiler_params=_params(("arbitrary", "arbitrary")),
        name="inproj",
    )(h, g, w_bf16)


def _mixer_body(proj_ref, halo_ref, convw_ref, wpool_ref, pscale_ref, lng_ref, lnb_ref,
                wsp_ref, bsp_ref, gout_ref, o_ref, za_scr, zb_scr, *, tiles_per_seq, d_conv, d_pool, d_sgu):
    t_rows = proj_ref.shape[0]
    i = pl.program_id(0)
    seq_tile = i % tiles_per_seq
    keep = (seq_tile > 0).astype(F32)

    o_za, o_ba, o_ca = 0, d_conv, 2 * d_conv
    o_zb = 3 * d_conv
    o_u = o_zb + d_pool
    o_v = o_u + d_sgu

    zc = proj_ref[:, o_ca:o_ca + d_conv].astype(F32) * proj_ref[:, o_za:o_za + d_conv].astype(F32)
    zc_halo = (halo_ref[:, o_ca:o_ca + d_conv].astype(F32) * halo_ref[:, o_za:o_za + d_conv].astype(F32)) * keep
    za_scr[0:HALO, :] = zc_halo
    za_scr[HALO:HALO + t_rows, :] = zc
    conv = zc * convw_ref[CONV_WIDTH - 1:CONV_WIDTH, :]
    for k in range(1, CONV_WIDTH):
        conv = conv + za_scr[HALO - k:HALO - k + t_rows, :] * convw_ref[CONV_WIDTH - 1 - k:CONV_WIDTH - k, :]
    y_a = proj_ref[:, o_ba:o_ba + d_conv].astype(F32) * conv

    zb = proj_ref[:, o_zb:o_zb + d_pool].astype(F32)
    zb_scr[0:HALO, :] = halo_ref[:, o_zb:o_zb + d_pool].astype(F32) * keep
    zb_scr[HALO:HALO + t_rows, :] = zb
    t_pos = seq_tile * t_rows + lax.broadcasted_iota(jnp.int32, (t_rows, 1), 0) + 1
    gdim = d_pool // len(POOL_WINDOWS)
    yb_parts = []
    for g, win in enumerate(POOL_WINDOWS):
        c0 = g * gdim
        s = zb[:, c0:c0 + gdim]
        for k in range(1, win):
            s = s + zb_scr[HALO - k:HALO - k + t_rows, c0:c0 + gdim]
        count = jnp.minimum(t_pos, win).astype(F32)
        pooled = s / count - zb[:, c0:c0 + gdim]
        yb_parts.append(jnp.dot(pooled.astype(BF16), wpool_ref[g].astype(BF16), preferred_element_type=F32))
    y_b = jnp.concatenate(yb_parts, axis=-1) * pscale_ref[...]

    v = proj_ref[:, o_v:o_v + d_sgu].astype(F32)
    mu = jnp.mean(v, axis=-1, keepdims=True)
    vc = v - mu
    var = jnp.mean(vc * vc, axis=-1, keepdims=True)
    vn = (vc * lax.rsqrt(var + EPS) * lng_ref[...] + lnb_ref[...]).astype(BF16)
    n_heads = d_sgu // HEAD_DIM
    row = lax.broadcasted_iota(jnp.int32, (CHUNK, CHUNK), 0)
    col = lax.broadcasted_iota(jnp.int32, (CHUNK, CHUNK), 1)
    yc_rows = []
    for c in range(t_rows // CHUNK):
        heads = []
        for hd in range(n_heads):
            ws = jnp.where(row >= col, wsp_ref[hd], 0.0).astype(BF16)
            vch = vn[c * CHUNK:(c + 1) * CHUNK, hd * HEAD_DIM:(hd + 1) * HEAD_DIM]
            heads.append(jnp.dot(ws, vch, preferred_element_type=F32) + bsp_ref[:, hd:hd + 1])
        yc_rows.append(jnp.concatenate(heads, axis=-1))
    y_c = proj_ref[:, o_u:o_u + d_sgu].astype(F32) * jnp.concatenate(yc_rows, axis=0)

    o_ref[:, 0:d_conv] = (_rms(y_a) * gout_ref[:, 0:d_conv]).astype(o_ref.dtype)
    o_ref[:, d_conv:d_conv + d_pool] = (_rms(y_b) * gout_ref[:, d_conv:d_conv + d_pool]).astype(o_ref.dtype)
    o_ref[:, d_conv + d_pool:] = (_rms(y_c) * gout_ref[:, d_conv + d_pool:]).astype(o_ref.dtype)


def _mixers(proj, conv_w, w_pool, pool_scale, ln_g, ln_b, w_spatial, b_spatial_t, g_out, *, seq, t_rows):
    n, d_in = proj.shape
    d_conv = conv_w.shape[1]
    d_pool = pool_scale.shape[1]
    d_sgu = ln_g.shape[1]
    d_mix = g_out.shape[1]
    halo_blocks = t_rows // HALO
    whole = lambda a: pl.BlockSpec(a.shape, lambda i: (0,) * a.ndim)
    body = functools.partial(_mixer_body, tiles_per_seq=seq // t_rows, d_conv=d_conv, d_pool=d_pool, d_sgu=d_sgu)
    return pl.pallas_call(
        body,
        grid=(n // t_rows,),
        in_specs=[
            pl.BlockSpec((t_rows, d_in), lambda i: (i, 0)),
            pl.BlockSpec((HALO, d_in), lambda i: (jnp.maximum(i * halo_blocks - 1, 0), 0)),
            whole(conv_w), whole(w_pool), whole(pool_scale), whole(ln_g), whole(ln_b),
            whole(w_spatial), whole(b_spatial_t), whole(g_out),
        ],
        out_specs=pl.BlockSpec((t_rows, d_mix), lambda i: (i, 0)),
        out_shape=jax.ShapeDtypeStruct((n, d_mix), BF16),
        scratch_shapes=[pltpu.VMEM((HALO + t_rows, d_conv), F32), pltpu.VMEM((HALO + t_rows, d_pool), F32)],
        compiler_params=_params(("arbitrary",)),
        name="mixers",
    )(proj, proj, conv_w, w_pool, pool_scale, ln_g, ln_b, w_spatial, b_spatial_t, g_out)


def _outproj_router_body(y_ref, h_ref, w_ref, g_ref, wrh_ref, wrl_ref, br_ref, h1_ref, xn_ref, idx_ref, prob_ref):
    h1 = h_ref[...] + jnp.dot(y_ref[...], w_ref[...], preferred_element_type=F32)
    h1_ref[...] = h1
    xn = _rms(h1) * g_ref[...]
    xn_ref[...] = xn
    x_hi = xn.astype(BF16)
    x_lo = (xn - x_hi.astype(F32)).astype(BF16)
    logits = (jnp.dot(x_hi, wrh_ref[...], preferred_element_type=F32)
              + (jnp.dot(x_lo, wrh_ref[...], preferred_element_type=F32)
                 + jnp.dot(x_hi, wrl_ref[...], preferred_element_type=F32))) + br_ref[...]
    lane = lax.broadcasted_iota(jnp.int32, logits.shape, 1)
    vals, idxs = [], []
    for _ in range(TOP_K):
        m = jnp.max(logits, axis=-1, keepdims=True)
        sel = jnp.min(jnp.where(logits == m, lane, ROUTER_LANES), axis=-1, keepdims=True)
        vals.append(m)
        idxs.append(sel)
        logits = jnp.where(lane == sel, -jnp.inf, logits)
    exps = [jnp.exp(v - vals[0]) for v in vals]
    denom = exps[0]
    for e in exps[1:]:
        denom = denom + e
    idx_out = jnp.zeros(lane.shape, jnp.int32)
    prob_out = jnp.zeros(lane.shape, F32)
    for k in range(TOP_K):
        idx_out = jnp.where(lane == k, idxs[k], idx_out)
        prob_out = jnp.where(lane == k, exps[k] / denom, prob_out)
    idx_ref[...] = idx_out
    prob_ref[...] = prob_out


def _outproj_router(y, h, w_out_bf16, g_ffn, w_router_hi, w_router_lo, b_router_pad, *, tm):
    n, d = h.shape
    whole = lambda a: pl.BlockSpec(a.shape, lambda i: (0,) * a.ndim)
    row_block = lambda width: pl.BlockSpec((tm, width), lambda i: (i, 0))
    return pl.pallas_call(
        _outproj_router_body,
        grid=(n // tm,),
        in_specs=[row_block(y.shape[1]), row_block(d), whole(w_out_bf16), whole(g_ffn),
                  whole(w_router_hi), whole(w_router_lo), whole(b_router_pad)],
        out_specs=[row_block(d), row_block(d), row_block(ROUTER_LANES), row_block(ROUTER_LANES)],
        out_shape=[jax.ShapeDtypeStruct((n, d), F32), jax.ShapeDtypeStruct((n, d), F32),
                   jax.ShapeDtypeStruct((n, ROUTER_LANES), jnp.int32),
                   jax.ShapeDtypeStruct((n, ROUTER_LANES), F32)],
        compiler_params=_params(("arbitrary",)),
        name="outproj_router",
    )(y, h, w_out_bf16, g_ffn, w_router_hi, w_router_lo, b_router_pad)


def _routing_plan(top_idx, n_experts):
    n = top_idx.shape[0]
    max_subs = (n * TOP_K) // SUB + n_experts
    max_tiles = max_subs // TILE_SUBS + n_experts
    hit = (top_idx[:, :, None] == jnp.arange(n_experts, dtype=jnp.int32)).any(axis=1).astype(jnp.int32)
    before = jnp.cumsum(hit, axis=0) - hit
    counts = jnp.sum(hit, axis=0)
    rank = jnp.take_along_axis(before, top_idx, axis=1)
    nsub = (counts + SUB - 1) // SUB
    sub_end = jnp.cumsum(nsub)
    sub_off = sub_end - nsub
    pos = sub_off[top_idx] * SUB + rank
    block = jnp.arange(max_subs, dtype=jnp.int32)
    is_last = ((block[:, None] == (sub_end - 1)[None, :]) & (nsub > 0)[None, :]).any(axis=1)
    pad_block = (is_last | (block >= sub_end[-1])).astype(jnp.int32)
    ntile = (nsub + TILE_SUBS - 1) // TILE_SUBS
    tile_end = jnp.cumsum(ntile)
    tile_off = tile_end - ntile
    n_tiles = tile_end[-1]
    t = jnp.arange(max_tiles, dtype=jnp.int32)
    t_eff = jnp.minimum(t, n_tiles - 1)
    expert = jnp.sum((tile_end[None, :] <= t_eff[:, None]).astype(jnp.int32), axis=1)
    within = t_eff - tile_off[expert]
    start_sub = sub_off[expert] + within * TILE_SUBS
    tile_nsub = jnp.where(t < n_tiles, jnp.minimum(TILE_SUBS, nsub[expert] - within * TILE_SUBS), 0)
    return dict(pos=pos, tile_expert=expert, tile_nsub=tile_nsub.astype(jnp.int32),
                tile_start=start_sub.astype(jnp.int32), used_subs=sub_end[-1:].astype(jnp.int32),
                pad_block=pad_block,
                n_pad_blocks=jnp.sum(pad_block).reshape(1), n_rows=max_subs * SUB)


def _pos_tiles(pos, tp):
    n = pos.shape[0]
    return pos.reshape(n // tp, tp, TOP_K).transpose(0, 2, 1).reshape(n // tp, 1, TOP_K * tp)


def _dispatch_body(pad_ref, npad_ref, pos_ref, x_ref, xs_hbm, zeros, sem):
    i = pl.program_id(0)
    tp, d = x_ref.shape

    def block_copy(b):
        return pltpu.make_async_copy(zeros, xs_hbm.at[pl.ds(pl.multiple_of(b * SUB, SUB), SUB)], sem)

    @pl.when(i == 0)
    def _():
        zeros[...] = jnp.zeros(zeros.shape, zeros.dtype)

        def push(b, carry):
            @pl.when(pad_ref[b] > 0)
            def _():
                block_copy(b).start()
            return carry
        lax.fori_loop(0, pad_ref.shape[0], push, 0)

        def drain(b, carry):
            block_copy(0).wait()
            return carry
        lax.fori_loop(0, npad_ref[0], drain, 0)

    def issue(c, carry):
        for u in range(ISSUE_UNROLL):
            r = c * ISSUE_UNROLL + u
            for k in range(TOP_K):
                pltpu.make_async_copy(x_ref.at[pl.ds(r, 1)], xs_hbm.at[pl.ds(pos_ref[0, k * tp + r], 1)],
                                      sem).start(priority=k % 2)
        return carry
    lax.fori_loop(0, tp // ISSUE_UNROLL, issue, 0)
    for k in range(TOP_K):
        pltpu.make_async_copy(x_ref, xs_hbm.at[pl.ds(0, tp)], sem).wait()


def _dispatch(xn, plan, *, tp):
    n, d = xn.shape
    grid_spec = pltpu.PrefetchScalarGridSpec(
        num_scalar_prefetch=2,
        grid=(n // tp,),
        in_specs=[
            pl.BlockSpec((None, 1, TOP_K * tp), lambda i, pad, npad: (i, 0, 0), memory_space=pltpu.SMEM),
            pl.BlockSpec((tp, d), lambda i, pad, npad: (i, 0)),
        ],
        out_specs=pl.BlockSpec(memory_space=pl.ANY),
        scratch_shapes=[pltpu.VMEM((SUB, d), xn.dtype), pltpu.SemaphoreType.DMA],
    )
    return pl.pallas_call(
        _dispatch_body,
        grid_spec=grid_spec,
        out_shape=jax.ShapeDtypeStruct((plan["n_rows"], d), xn.dtype),
        compiler_params=_params(("arbitrary",)),
        name="dispatch",
    )(plan["pad_block"], plan["n_pad_blocks"], _pos_tiles(plan["pos"], tp), xn)


def _row_copy(src_hbm, dst_vmem, sem, src_row, dst_row):
    return pltpu.make_async_copy(src_hbm.at[pl.ds(src_row, 1)], dst_vmem.at[pl.ds(dst_row, 1)], sem)


def _moe_body(expert_ref, nsub_ref, start_ref, used_ref, xs_hbm, wg_ref, bg_ref, wu_ref, bu_ref, wd_ref, bd_ref,
              ys_hbm, xg, xb, acc, wgb, wub, wdb, gsem, osem, *, n_t, n_f):
    del expert_ref
    t = pl.program_id(0)
    f = pl.program_id(1)
    ns = nsub_ref[t]
    t_next = jnp.minimum(t + 1, n_t - 1)
    ns_next = jnp.where(t + 1 < n_t, nsub_ref[t_next], 0)

    def sub_rows(j):
        return pl.ds(pl.multiple_of(j * SUB, SUB), SUB)

    def in_copy(tile, j):
        src = pl.ds(pl.multiple_of((start_ref[tile] + j) * SUB, SUB), SUB)
        return pltpu.make_async_copy(xs_hbm.at[src], xg.at[sub_rows(j)], gsem)

    def out_copy(tile, j):
        dst = pl.ds(pl.multiple_of((start_ref[tile] + j) * SUB, SUB), SUB)
        return pltpu.make_async_copy(acc.at[sub_rows(j)], ys_hbm.at[dst], osem)

    @pl.when((t == 0) & (f == 0))
    def _():
        used = used_ref[0]
        n_tail = ys_hbm.shape[0] // SUB - used
        xg[0:SUB, :] = jnp.zeros((SUB, xg.shape[1]), F32)

        def tail_copy(j):
            dst = pl.ds(pl.multiple_of((used + j) * SUB, SUB), SUB)
            return pltpu.make_async_copy(xg.at[pl.ds(0, SUB)], ys_hbm.at[dst], osem)

        def push(j, carry):
            tail_copy(j).start()
            return carry
        lax.fori_loop(0, n_tail, push, 0)

        def drain(j, carry):
            tail_copy(j).wait()
            return carry
        lax.fori_loop(0, n_tail, drain, 0)

    @pl.when((f == 0) & (t > 0))
    def _():
        def drain(j, carry):
            out_copy(t - 1, j).wait()
            return carry
        lax.fori_loop(0, nsub_ref[jnp.maximum(t - 1, 0)], drain, 0)

    @pl.when(ns > 0)
    def _():
        @pl.when(f == 0)
        def _():
            @pl.when(t == 0)
            def _():
                def fetch(j, carry):
                    in_copy(t, j).start()
                    return carry
                lax.fori_loop(0, ns, fetch, 0)

            def land(j, carry):
                in_copy(t, j).wait()
                return carry
            lax.fori_loop(0, ns, land, 0)

            def narrow(j, carry):
                xb[sub_rows(j), :] = xg[sub_rows(j), :].astype(BF16)
                acc[sub_rows(j), :] = jnp.broadcast_to(bd_ref[...], (SUB, acc.shape[1]))
                return carry
            lax.fori_loop(0, ns, narrow, 0)

            def prefetch(j, carry):
                in_copy(t_next, j).start()
                return carry
            lax.fori_loop(0, ns_next, prefetch, 0)

        wgb[...] = wg_ref[...].astype(BF16)
        wub[...] = wu_ref[...].astype(BF16)
        wdb[...] = wd_ref[...].astype(BF16)

        def ffn(j):
            x = xb[sub_rows(j), :]
            g = jnp.minimum(jnp.dot(x, wgb[...], preferred_element_type=F32) + bg_ref[...], SWIGLU_LIMIT)
            u = jnp.clip(jnp.dot(x, wub[...], preferred_element_type=F32) + bu_ref[...], -SWIGLU_LIMIT, SWIGLU_LIMIT)
            a = (u + 1.0) * g * jax.nn.sigmoid(SWIGLU_ALPHA * g)
            acc[sub_rows(j), :] += jnp.dot(a.astype(BF16), wdb[...], preferred_element_type=F32)

        def ffn_pair(p, carry):
            ffn(2 * p)
            ffn(2 * p + 1)
            return carry
        lax.fori_loop(0, lax.shift_right_logical(ns, 1), ffn_pair, 0)

        @pl.when((ns & 1) == 1)
        def _():
            ffn(ns - 1)

        @pl.when(f == n_f - 1)
        def _():
            def push(j, carry):
                out_copy(t, j).start()
                return carry
            lax.fori_loop(0, ns, push, 0)

            @pl.when(t == n_t - 1)
            def _():
                def drain(j, carry):
                    out_copy(t, j).wait()
                    return carry
                lax.fori_loop(0, ns, drain, 0)


def _moe_ffn(xs, plan, layer, w_gate, b_gate, w_up, b_up, w_down, b_down):
    d = xs.shape[1]
    depth, n_experts, _, d_ff = w_gate.shape
    tile_rows = SUB * TILE_SUBS
    max_tiles = plan["tile_expert"].shape[0]
    n_f = d_ff // FF_BLOCK

    def ff_block(f, nsub_ref, t):
        return jnp.where(nsub_ref[t] > 0, f, n_f - 1)

    col_w = pl.BlockSpec((None, None, d, FF_BLOCK),
                         lambda t, f, e, ns, st, used: (layer, e[t], 0, ff_block(f, ns, t)))
    col_b = pl.BlockSpec((None, None, 1, FF_BLOCK),
                         lambda t, f, e, ns, st, used: (layer, e[t], 0, ff_block(f, ns, t)))
    grid_spec = pltpu.PrefetchScalarGridSpec(
        num_scalar_prefetch=4,
        grid=(max_tiles, n_f),
        in_specs=[
            pl.BlockSpec(memory_space=pl.ANY),
            col_w, col_b, col_w, col_b,
            pl.BlockSpec((None, None, FF_BLOCK, d),
                         lambda t, f, e, ns, st, used: (layer, e[t], ff_block(f, ns, t), 0)),
            pl.BlockSpec((None, None, 1, d), lambda t, f, e, ns, st, used: (layer, e[t], 0, 0)),
        ],
        out_specs=pl.BlockSpec(memory_space=pl.ANY),
        scratch_shapes=[
            pltpu.VMEM((tile_rows, d), F32), pltpu.VMEM((tile_rows, d), BF16), pltpu.VMEM((tile_rows, d), F32),
            pltpu.VMEM((d, FF_BLOCK), BF16), pltpu.VMEM((d, FF_BLOCK), BF16), pltpu.VMEM((FF_BLOCK, d), BF16),
            pltpu.SemaphoreType.DMA, pltpu.SemaphoreType.DMA,
        ],
    )
    return pl.pallas_call(
        functools.partial(_moe_body, n_t=max_tiles, n_f=n_f),
        grid_spec=grid_spec,
        out_shape=jax.ShapeDtypeStruct((plan["n_rows"], d), F32),
        compiler_params=_params(("arbitrary", "arbitrary")),
        name="moe_ffn",
    )(plan["tile_expert"], plan["tile_nsub"], plan["tile_start"], plan["used_subs"], xs,
      w_gate, b_gate.reshape(depth, n_experts, 1, d_ff), w_up, b_up.reshape(depth, n_experts, 1, d_ff),
      w_down, b_down.reshape(depth, n_experts, 1, d))


def _ple_body(pos_ref, prob_ref, h_ref, ys_hbm, p_ref, gin_ref, wg_ref, wp_ref, gpost_ref, gfin_ref, o_ref,
              yg, sem, *, final):
    tp = h_ref.shape[0]

    def issue(r, carry):
        for k in range(TOP_K):
            _row_copy(ys_hbm, yg.at[k], sem, pos_ref[0, k * tp + r], r).start(priority=k % 2)
        return carry
    lax.fori_loop(0, tp, issue, 0, unroll=2)
    for k in range(TOP_K):
        pltpu.make_async_copy(ys_hbm.at[pl.ds(0, tp)], yg.at[k], sem).wait()

    h2 = h_ref[...]
    for k in range(TOP_K):
        h2 = h2 + prob_ref[:, k:k + 1] * yg[k]
    xn = (_rms(h2) * gin_ref[...]).astype(BF16)
    gate = jax.nn.sigmoid(jnp.dot(xn, wg_ref[...], preferred_element_type=F32))
    ple = _rms(jnp.dot(p_ref[...].astype(BF16), wp_ref[...], preferred_element_type=F32)) * gpost_ref[...]
    h3 = h2 + gate * ple
    if final:
        h3 = _rms(h3) * gfin_ref[...]
    o_ref[...] = h3


def _combine_ple(pos, probs, h1, ys, p, g_in, w_gate_bf16, w_proj_bf16, g_post, g_final, *, tp, final):
    n, d = h1.shape
    d_ple = p.shape[1]
    pos_tiles = _pos_tiles(pos, tp)
    whole = lambda a: pl.BlockSpec(a.shape, lambda i: (0,) * a.ndim)
    row_block = lambda width: pl.BlockSpec((tp, width), lambda i: (i, 0))
    return pl.pallas_call(
        functools.partial(_ple_body, final=final),
        grid=(n // tp,),
        in_specs=[
            pl.BlockSpec((None, 1, TOP_K * tp), lambda i: (i, 0, 0), memory_space=pltpu.SMEM),
            row_block(TOP_K), row_block(d), pl.BlockSpec(memory_space=pl.ANY), row_block(d_ple),
            whole(g_in), whole(w_gate_bf16), whole(w_proj_bf16), whole(g_post), whole(g_final),
        ],
        out_specs=row_block(d),
        out_shape=jax.ShapeDtypeStruct((n, d), F32),
        scratch_shapes=[pltpu.VMEM((TOP_K, tp, d), F32), pltpu.SemaphoreType.DMA],
        compiler_params=_params(("arbitrary",)),
        name="combine_ple",
    )(pos_tiles, probs, h1, ys, p, g_in, w_gate_bf16, w_proj_bf16, g_post, g_final)


def kernel(x, p, g_mix_norm, w_in, conv_w, w_pool, pool_scale, ln_v_g, ln_v_b, w_spatial, b_spatial, g_out,
           w_out, g_ffn_norm, w_router, b_router, w_gate, b_gate, w_up, b_up, w_down, b_down, g_ple_in,
           w_ple_gate, w_ple_proj, g_ple_post, g_final):
    batch, seq, d = x.shape
    depth = w_in.shape[0]
    n = batch * seq
    n_experts = w_router.shape[2]
    row = lambda v: v.reshape(1, -1)

    h = x.reshape(n, d)
    for i in range(depth):
        proj = _inproj(h, row(g_mix_norm[i]), w_in[i].astype(BF16), tm=512, tn=w_in.shape[2] // 2)
        y = _mixers(proj, conv_w[i], w_pool[i], row(pool_scale[i]), row(ln_v_g[i]), row(ln_v_b[i]),
                    w_spatial[i], b_spatial[i].T, row(g_out[i]), seq=seq, t_rows=256)
        w_router_pad = jnp.pad(w_router[i], ((0, 0), (0, ROUTER_LANES - n_experts)))
        w_router_hi = w_router_pad.astype(BF16)
        w_router_lo = (w_router_pad - w_router_hi.astype(F32)).astype(BF16)
        b_router_pad = jnp.pad(row(b_router[i]), ((0, 0), (0, ROUTER_LANES - n_experts)), constant_values=-jnp.inf)
        h1, xn, idx_pad, prob_pad = _outproj_router(y, h, w_out[i].astype(BF16), row(g_ffn_norm[i]),
                                                    w_router_hi, w_router_lo, b_router_pad, tm=256)
        top_idx, probs = idx_pad[:, :TOP_K], prob_pad[:, :TOP_K]
        plan = _routing_plan(top_idx, n_experts)
        xs = _dispatch(xn, plan, tp=256)
        ys = _moe_ffn(xs, plan, i, w_gate, b_gate, w_up, b_up, w_down, b_down)
        h = _combine_ple(plan["pos"], probs, h1, ys, p[i].reshape(n, -1), row(g_ple_in[i]),
                         w_ple_gate[i].astype(BF16), w_ple_proj[i].astype(BF16), row(g_ple_post[i]),
                         row(g_final), tp=256, final=(i == depth - 1))
    return h.reshape(batch, seq, d)
```

```python
import functools

import jax
import jax.numpy as jnp
from jax import lax
from jax.experimental import pallas as pl
from jax.experimental.pallas import tpu as pltpu

F32 = jnp.float32
BF16 = jnp.bfloat16

EPS = 1e-6
HEAD_DIM = 128
CHUNK = 128
CONV_WIDTH = 3
POOL_WINDOWS = (2, 4, 8, 16)
TOP_K = 4
SWIGLU_LIMIT = 7.0
SWIGLU_ALPHA = 1.702

HALO = 16
ROUTER_LANES = 128
VMEM_LIMIT = 56 * 1024 * 1024

SUB = 256
TILE_SUBS = 5
FF_BLOCK = 256
ISSUE_UNROLL = 8


def _rms(x):
    return x * lax.rsqrt(jnp.mean(x * x, axis=-1, keepdims=True) + EPS)


def _params(semantics):
    return pltpu.CompilerParams(dimension_semantics=semantics, vmem_limit_bytes=VMEM_LIMIT)


def _inproj_body(h_ref, g_ref, w_ref, o_ref):
    xn = _rms(h_ref[...]) * g_ref[...]
    o_ref[...] = jnp.dot(xn.astype(BF16), w_ref[...], preferred_element_type=F32).astype(o_ref.dtype)


def _inproj(h, g, w_bf16, *, tm, tn):
    n, d = h.shape
    d_in = w_bf16.shape[1]
    return pl.pallas_call(
        _inproj_body,
        grid=(d_in // tn, n // tm),
        in_specs=[
            pl.BlockSpec((tm, d), lambda j, i: (i, 0)),
            pl.BlockSpec((1, d), lambda j, i: (0, 0)),
            pl.BlockSpec((d, tn), lambda j, i: (0, j)),
        ],
        out_specs=pl.BlockSpec((tm, tn), lambda j, i: (i, j)),
        out_shape=jax.ShapeDtypeStruct((n, d_in), BF16),
        compiler_params=_params(("arbitrary", "arbitrary")),
        name="inproj",
    )(h, g, w_bf16)


def _mixer_body(proj_ref, halo_ref, convw_ref, wpool_ref, pscale_ref, lng_ref, lnb_ref,
                wsp_ref, bsp_ref, gout_ref, o_ref, za_scr, zb_scr, *, tiles_per_seq, d_conv, d_pool, d_sgu):
    t_rows = proj_ref.shape[0]
    i = pl.program_id(0)
    seq_tile = i % tiles_per_seq
    keep = (seq_tile > 0).astype(F32)

    o_za, o_ba, o_ca = 0, d_conv, 2 * d_conv
    o_zb = 3 * d_conv
    o_u = o_zb + d_pool
    o_v = o_u + d_sgu

    zc = proj_ref[:, o_ca:o_ca + d_conv].astype(F32) * proj_ref[:, o_za:o_za + d_conv].astype(F32)
    zc_halo = (halo_ref[:, o_ca:o_ca + d_conv].astype(F32) * halo_ref[:, o_za:o_za + d_conv].astype(F32)) * keep
    za_scr[0:HALO, :] = zc_halo
    za_scr[HALO:HALO + t_rows, :] = zc
    conv = zc * convw_ref[CONV_WIDTH - 1:CONV_WIDTH, :]
    for k in range(1, CONV_WIDTH):
        conv = conv + za_scr[HALO - k:HALO - k + t_rows, :] * convw_ref[CONV_WIDTH - 1 - k:CONV_WIDTH - k, :]
    y_a = proj_ref[:, o_ba:o_ba + d_conv].astype(F32) * conv

    zb = proj_ref[:, o_zb:o_zb + d_pool].astype(F32)
    zb_scr[0:HALO, :] = halo_ref[:, o_zb:o_zb + d_pool].astype(F32) * keep
    zb_scr[HALO:HALO + t_rows, :] = zb
    t_pos = seq_tile * t_rows + lax.broadcasted_iota(jnp.int32, (t_rows, 1), 0) + 1
    gdim = d_pool // len(POOL_WINDOWS)
    yb_parts = []
    for g, win in enumerate(POOL_WINDOWS):
        c0 = g * gdim
        s = zb[:, c0:c0 + gdim]
        for k in range(1, win):
            s = s + zb_scr[HALO - k:HALO - k + t_rows, c0:c0 + gdim]
        count = jnp.minimum(t_pos, win).astype(F32)
        pooled = s / count - zb[:, c0:c0 + gdim]
        yb_parts.append(jnp.dot(pooled.astype(BF16), wpool_ref[g].astype(BF16), preferred_element_type=F32))
    y_b = jnp.concatenate(yb_parts, axis=-1) * pscale_ref[...]

    v = proj_ref[:, o_v:o_v + d_sgu].astype(F32)
    mu = jnp.mean(v, axis=-1, keepdims=True)
    vc = v - mu
    var = jnp.mean(vc * vc, axis=-1, keepdims=True)
    vn = (vc * lax.rsqrt(var + EPS) * lng_ref[...] + lnb_ref[...]).astype(BF16)
    n_heads = d_sgu // HEAD_DIM
    row = lax.broadcasted_iota(jnp.int32, (CHUNK, CHUNK), 0)
    col = lax.broadcasted_iota(jnp.int32, (CHUNK, CHUNK), 1)
    yc_rows = []
    for c in range(t_rows // CHUNK):
        heads = []
        for hd in range(n_heads):
            ws = jnp.where(row >= col, wsp_ref[hd], 0.0).astype(BF16)
            vch = vn[c * CHUNK:(c + 1) * CHUNK, hd * HEAD_DIM:(hd + 1) * HEAD_DIM]
            heads.append(jnp.dot(ws, vch, preferred_element_type=F32) + bsp_ref[:, hd:hd + 1])
        yc_rows.append(jnp.concatenate(heads, axis=-1))
    y_c = proj_ref[:, o_u:o_u + d_sgu].astype(F32) * jnp.concatenate(yc_rows, axis=0)

    o_ref[:, 0:d_conv] = (_rms(y_a) * gout_ref[:, 0:d_conv]).astype(o_ref.dtype)
    o_ref[:, d_conv:d_conv + d_pool] = (_rms(y_b) * gout_ref[:, d_conv:d_conv + d_pool]).astype(o_ref.dtype)
    o_ref[:, d_conv + d_pool:] = (_rms(y_c) * gout_ref[:, d_conv + d_pool:]).astype(o_ref.dtype)


def _mixers(proj, conv_w, w_pool, pool_scale, ln_g, ln_b, w_spatial, b_spatial_t, g_out, *, seq, t_rows):
    n, d_in = proj.shape
    d_conv = conv_w.shape[1]
    d_pool = pool_scale.shape[1]
    d_sgu = ln_g.shape[1]
    d_mix = g_out.shape[1]
    halo_blocks = t_rows // HALO
    whole = lambda a: pl.BlockSpec(a.shape, lambda i: (0,) * a.ndim)
    body = functools.partial(_mixer_body, tiles_per_seq=seq // t_rows, d_conv=d_conv, d_pool=d_pool, d_sgu=d_sgu)
    return pl.pallas_call(
        body,
        grid=(n // t_rows,),
        in_specs=[
            pl.BlockSpec((t_rows, d_in), lambda i: (i, 0)),
            pl.BlockSpec((HALO, d_in), lambda i: (jnp.maximum(i * halo_blocks - 1, 0), 0)),
            whole(conv_w), whole(w_pool), whole(pool_scale), whole(ln_g), whole(ln_b),
            whole(w_spatial), whole(b_spatial_t), whole(g_out),
        ],
        out_specs=pl.BlockSpec((t_rows, d_mix), lambda i: (i, 0)),
        out_shape=jax.ShapeDtypeStruct((n, d_mix), BF16),
        scratch_shapes=[pltpu.VMEM((HALO + t_rows, d_conv), F32), pltpu.VMEM((HALO + t_rows, d_pool), F32)],
        compiler_params=_params(("arbitrary",)),
        name="mixers",
    )(proj, proj, conv_w, w_pool, pool_scale, ln_g, ln_b, w_spatial, b_spatial_t, g_out)


def _outproj_router_body(y_ref, h_ref, w_ref, g_ref, wrh_ref, wrl_ref, br_ref, h1_ref, xn_ref, idx_ref, prob_ref):
    h1 = h_ref[...] + jnp.dot(y_ref[...], w_ref[...], preferred_element_type=F32)
    h1_ref[...] = h1
    xn = _rms(h1) * g_ref[...]
    xn_ref[...] = xn
    x_hi = xn.astype(BF16)
    x_lo = (xn - x_hi.astype(F32)).astype(BF16)
    logits = (jnp.dot(x_hi, wrh_ref[...], preferred_element_type=F32)
              + (jnp.dot(x_lo, wrh_ref[...], preferred_element_type=F32)
                 + jnp.dot(x_hi, wrl_ref[...], preferred_element_type=F32))) + br_ref[...]
    lane = lax.broadcasted_iota(jnp.int32, logits.shape, 1)
    vals, idxs = [], []
    for _ in range(TOP_K):
        m = jnp.max(logits, axis=-1, keepdims=True)
        sel = jnp.min(jnp.where(logits == m, lane, ROUTER_LANES), axis=-1, keepdims=True)
        vals.append(m)
        idxs.append(sel)
        logits = jnp.where(lane == sel, -jnp.inf, logits)
    exps = [jnp.exp(v - vals[0]) for v in vals]
    denom = exps[0]
    for e in exps[1:]:
        denom = denom + e
    idx_out = jnp.zeros(lane.shape, jnp.int32)
    prob_out = jnp.zeros(lane.shape, F32)
    for k in range(TOP_K):
        idx_out = jnp.where(lane == k, idxs[k], idx_out)
        prob_out = jnp.where(lane == k, exps[k] / denom, prob_out)
    idx_ref[...] = idx_out
    prob_ref[...] = prob_out


def _outproj_router(y, h, w_out_bf16, g_ffn, w_router_hi, w_router_lo, b_router_pad, *, tm):
    n, d = h.shape
    whole = lambda a: pl.BlockSpec(a.shape, lambda i: (0,) * a.ndim)
    row_block = lambda width: pl.BlockSpec((tm, width), lambda i: (i, 0))
    return pl.pallas_call(
        _outproj_router_body,
        grid=(n // tm,),
        in_specs=[row_block(y.shape[1]), row_block(d), whole(w_out_bf16), whole(g_ffn),
                  whole(w_router_hi), whole(w_router_lo), whole(b_router_pad)],
        out_specs=[row_block(d), row_block(d), row_block(ROUTER_LANES), row_block(ROUTER_LANES)],
        out_shape=[jax.ShapeDtypeStruct((n, d), F32), jax.ShapeDtypeStruct((n, d), F32),
                   jax.ShapeDtypeStruct((n, ROUTER_LANES), jnp.int32),
                   jax.ShapeDtypeStruct((n, ROUTER_LANES), F32)],
        compiler_params=_params(("arbitrary",)),
        name="outproj_router",
    )(y, h, w_out_bf16, g_ffn, w_router_hi, w_router_lo, b_router_pad)


def _routing_plan(top_idx, n_experts):
    n = top_idx.shape[0]
    max_subs = (n * TOP_K) // SUB + n_experts
    max_tiles = max_subs // TILE_SUBS + n_experts
    hit = (top_idx[:, :, None] == jnp.arange(n_experts, dtype=jnp.int32)).any(axis=1).astype(jnp.int32)
    before = jnp.cumsum(hit, axis=0) - hit
    counts = jnp.sum(hit, axis=0)
    rank = jnp.take_along_axis(before, top_idx, axis=1)
    nsub = (counts + SUB - 1) // SUB
    sub_end = jnp.cumsum(nsub)
    sub_off = sub_end - nsub
    pos = sub_off[top_idx] * SUB + rank
    block = jnp.arange(max_subs, dtype=jnp.int32)
    is_last = ((block[:, None] == (sub_end - 1)[None, :]) & (nsub > 0)[None, :]).any(axis=1)
    pad_block = (is_last | (block >= sub_end[-1])).astype(jnp.int32)
    ntile = (nsub + TILE_SUBS - 1) // TILE_SUBS
    tile_end = jnp.cumsum(ntile)
    tile_off = tile_end - ntile
    n_tiles = tile_end[-1]
    t = jnp.arange(max_tiles, dtype=jnp.int32)
    t_eff = jnp.minimum(t, n_tiles - 1)
    expert = jnp.sum((tile_end[None, :] <= t_eff[:, None]).astype(jnp.int32), axis=1)
    within = t_eff - tile_off[expert]
    start_sub = sub_off[expert] + within * TILE_SUBS
    tile_nsub = jnp.where(t < n_tiles, jnp.minimum(TILE_SUBS, nsub[expert] - within * TILE_SUBS), 0)
    return dict(pos=pos, tile_expert=expert, tile_nsub=tile_nsub.astype(jnp.int32),
                tile_start=start_sub.astype(jnp.int32), used_subs=sub_end[-1:].astype(jnp.int32),
                pad_block=pad_block,
                n_pad_blocks=jnp.sum(pad_block).reshape(1), n_rows=max_subs * SUB)


def _pos_tiles(pos, tp):
    n = pos.shape[0]
    return pos.reshape(n // tp, tp, TOP_K).transpose(0, 2, 1).reshape(n // tp, 1, TOP_K * tp)


def _dispatch_body(pad_ref, npad_ref, pos_ref, x_ref, xs_hbm, zeros, sem):
    i = pl.program_id(0)
    tp, d = x_ref.shape

    def block_copy(b):
        return pltpu.make_async_copy(zeros, xs_hbm.at[pl.ds(pl.multiple_of(b * SUB, SUB), SUB)], sem)

    @pl.when(i == 0)
    def _():
        zeros[...] = jnp.zeros(zeros.shape, zeros.dtype)

        def push(b, carry):
            @pl.when(pad_ref[b] > 0)
            def _():
                block_copy(b).start()
            return carry
        lax.fori_loop(0, pad_ref.shape[0], push, 0)

        def drain(b, carry):
            block_copy(0).wait()
            return carry
        lax.fori_loop(0, npad_ref[0], drain, 0)

    def issue(c, carry):
        for u in range(ISSUE_UNROLL):
            r = c * ISSUE_UNROLL + u
            for k in range(TOP_K):
                pltpu.make_async_copy(x_ref.at[pl.ds(r, 1)], xs_hbm.at[pl.ds(pos_ref[0, k * tp + r], 1)],
                                      sem).start(priority=k % 2)
        return carry
    lax.fori_loop(0, tp // ISSUE_UNROLL, issue, 0)
    for k in range(TOP_K):
        pltpu.make_async_copy(x_ref, xs_hbm.at[pl.ds(0, tp)], sem).wait()


def _dispatch(xn, plan, *, tp):
    n, d = xn.shape
    grid_spec = pltpu.PrefetchScalarGridSpec(
        num_scalar_prefetch=2,
        grid=(n // tp,),
        in_specs=[
            pl.BlockSpec((None, 1, TOP_K * tp), lambda i, pad, npad: (i, 0, 0), memory_space=pltpu.SMEM),
            pl.BlockSpec((tp, d), lambda i, pad, npad: (i, 0)),
        ],
        out_specs=pl.BlockSpec(memory_space=pl.ANY),
        scratch_shapes=[pltpu.VMEM((SUB, d), xn.dtype), pltpu.SemaphoreType.DMA],
    )
    return pl.pallas_call(
        _dispatch_body,
        grid_spec=grid_spec,
        out_shape=jax.ShapeDtypeStruct((plan["n_rows"], d), xn.dtype),
        compiler_params=_params(("arbitrary",)),
        name="dispatch",
    )(plan["pad_block"], plan["n_pad_blocks"], _pos_tiles(plan["pos"], tp), xn)


def _row_copy(src_hbm, dst_vmem, sem, src_row, dst_row):
    return pltpu.make_async_copy(src_hbm.at[pl.ds(src_row, 1)], dst_vmem.at[pl.ds(dst_row, 1)], sem)


def _moe_body(expert_ref, nsub_ref, start_ref, used_ref, xs_hbm, wg_ref, bg_ref, wu_ref, bu_ref, wd_ref, bd_ref,
              ys_hbm, xg, xb, acc, wgb, wub, wdb, gsem, osem, *, n_t, n_f):
    del expert_ref
    t = pl.program_id(0)
    f = pl.program_id(1)
    ns = nsub_ref[t]
    t_next = jnp.minimum(t + 1, n_t - 1)
    ns_next = jnp.where(t + 1 < n_t, nsub_ref[t_next], 0)

    def sub_rows(j):
        return pl.ds(pl.multiple_of(j * SUB, SUB), SUB)

    def in_copy(tile, j):
        src = pl.ds(pl.multiple_of((start_ref[tile] + j) * SUB, SUB), SUB)
        return pltpu.make_async_copy(xs_hbm.at[src], xg.at[sub_rows(j)], gsem)

    def out_copy(tile, j):
        dst = pl.ds(pl.multiple_of((start_ref[tile] + j) * SUB, SUB), SUB)
        return pltpu.make_async_copy(acc.at[sub_rows(j)], ys_hbm.at[dst], osem)

    @pl.when((t == 0) & (f == 0))
    def _():
        used = used_ref[0]
        n_tail = ys_hbm.shape[0] // SUB - used
        xg[0:SUB, :] = jnp.zeros((SUB, xg.shape[1]), F32)

        def tail_copy(j):
            dst = pl.ds(pl.multiple_of((used + j) * SUB, SUB), SUB)
            return pltpu.make_async_copy(xg.at[pl.ds(0, SUB)], ys_hbm.at[dst], osem)

        def push(j, carry):
            tail_copy(j).start()
            return carry
        lax.fori_loop(0, n_tail, push, 0)

        def drain(j, carry):
            tail_copy(j).wait()
            return carry
        lax.fori_loop(0, n_tail, drain, 0)

    @pl.when((f == 0) & (t > 0))
    def _():
        def drain(j, carry):
            out_copy(t - 1, j).wait()
            return carry
        lax.fori_loop(0, nsub_ref[jnp.maximum(t - 1, 0)], drain, 0)

    @pl.when(ns > 0)
    def _():
        @pl.when(f == 0)
        def _():
            @pl.when(t == 0)
            def _():
                def fetch(j, carry):
                    in_copy(t, j).start()
                    return carry
                lax.fori_loop(0, ns, fetch, 0)

            def land(j, carry):
                in_copy(t, j).wait()
                return carry
            lax.fori_loop(0, ns, land, 0)

            def narrow(j, carry):
                xb[sub_rows(j), :] = xg[sub_rows(j), :].astype(BF16)
                acc[sub_rows(j), :] = jnp.broadcast_to(bd_ref[...], (SUB, acc.shape[1]))
                return carry
            lax.fori_loop(0, ns, narrow, 0)

            def prefetch(j, carry):
                in_copy(t_next, j).start()
                return carry
            lax.fori_loop(0, ns_next, prefetch, 0)

        wgb[...] = wg_ref[...].astype(BF16)
        wub[...] = wu_ref[...].astype(BF16)
        wdb[...] = wd_ref[...].astype(BF16)

        def ffn(rows):
            x = xb[rows, :]
            g = jnp.minimum(jnp.dot(x, wgb[...], preferred_element_type=F32) + bg_ref[...], SWIGLU_LIMIT)
            u = jnp.clip(jnp.dot(x, wub[...], preferred_element_type=F32) + bu_ref[...], -SWIGLU_LIMIT, SWIGLU_LIMIT)
            a = (u + 1.0) * g * jax.nn.sigmoid(SWIGLU_ALPHA * g)
            acc[rows, :] += jnp.dot(a.astype(BF16), wdb[...], preferred_element_type=F32)

        def ffn_quad(q, carry):
            ffn(pl.ds(pl.multiple_of(q * (4 * SUB), 4 * SUB), 4 * SUB))
            return carry
        n_quad = lax.shift_right_logical(ns, 2)
        lax.fori_loop(0, n_quad, ffn_quad, 0)

        @pl.when((ns & 2) == 2)
        def _():
            ffn(pl.ds(pl.multiple_of(n_quad * (4 * SUB), 2 * SUB), 2 * SUB))

        @pl.when((ns & 1) == 1)
        def _():
            ffn(sub_rows(ns - 1))

        @pl.when(f == n_f - 1)
        def _():
            def push(j, carry):
                out_copy(t, j).start()
                return carry
            lax.fori_loop(0, ns, push, 0)

            @pl.when(t == n_t - 1)
            def _():
                def drain(j, carry):
                    out_copy(t, j).wait()
                    return carry
                lax.fori_loop(0, ns, drain, 0)


def _moe_ffn(xs, plan, layer, w_gate, b_gate, w_up, b_up, w_down, b_down):
    d = xs.shape[1]
    depth, n_experts, _, d_ff = w_gate.shape
    tile_rows = SUB * TILE_SUBS
    max_tiles = plan["tile_expert"].shape[0]
    n_f = d_ff // FF_BLOCK

    def ff_block(f, nsub_ref, t):
        return jnp.where(nsub_ref[t] > 0, f, n_f - 1)

    col_w = pl.BlockSpec((None, None, d, FF_BLOCK),
                         lambda t, f, e, ns, st, used: (layer, e[t], 0, ff_block(f, ns, t)))
    col_b = pl.BlockSpec((None, None, 1, FF_BLOCK),
                         lambda t, f, e, ns, st, used: (layer, e[t], 0, ff_block(f, ns, t)))
    grid_spec = pltpu.PrefetchScalarGridSpec(
        num_scalar_prefetch=4,
        grid=(max_tiles, n_f),
        in_specs=[
            pl.BlockSpec(memory_space=pl.ANY),
            col_w, col_b, col_w, col_b,
            pl.BlockSpec((None, None, FF_BLOCK, d),
                         lambda t, f, e, ns, st, used: (layer, e[t], ff_block(f, ns, t), 0)),
            pl.BlockSpec((None, None, 1, d), lambda t, f, e, ns, st, used: (layer, e[t], 0, 0)),
        ],
        out_specs=pl.BlockSpec(memory_space=pl.ANY),
        scratch_shapes=[
            pltpu.VMEM((tile_rows, d), F32), pltpu.VMEM((tile_rows, d), BF16), pltpu.VMEM((tile_rows, d), F32),
            pltpu.VMEM((d, FF_BLOCK), BF16), pltpu.VMEM((d, FF_BLOCK), BF16), pltpu.VMEM((FF_BLOCK, d), BF16),
            pltpu.SemaphoreType.DMA, pltpu.SemaphoreType.DMA,
        ],
    )
    return pl.pallas_call(
        functools.partial(_moe_body, n_t=max_tiles, n_f=n_f),
        grid_spec=grid_spec,
        out_shape=jax.ShapeDtypeStruct((plan["n_rows"], d), F32),
        compiler_params=_params(("arbitrary", "arbitrary")),
        name="moe_ffn",
    )(plan["tile_expert"], plan["tile_nsub"], plan["tile_start"], plan["used_subs"], xs,
      w_gate, b_gate.reshape(depth, n_experts, 1, d_ff), w_up, b_up.reshape(depth, n_experts, 1, d_ff),
      w_down, b_down.reshape(depth, n_experts, 1, d))


def _ple_body(pos_ref, next_pos_ref, prob_ref, h_ref, ys_hbm, p_ref, gin_ref, wg_ref, wp_ref, gpost_ref, gfin_ref,
              o_ref, yg, sems, *, final):
    tp = h_ref.shape[0]
    i = pl.program_id(0)
    n_i = pl.num_programs(0)
    slot = i & 1

    def fetch_row(idx_ref, dst_slot, r):
        for k in range(TOP_K):
            _row_copy(ys_hbm, yg.at[dst_slot, k], sems.at[dst_slot], idx_ref[0, k * tp + r], r).start(priority=k % 2)

    @pl.when(i == 0)
    def _():
        def issue(r, carry):
            fetch_row(pos_ref, 0, r)
            return carry
        lax.fori_loop(0, tp, issue, 0, unroll=2)

    for parity in range(2):
        @pl.when((i + 1 < n_i) & (slot == parity))
        def _():
            for r in range(tp):
                fetch_row(next_pos_ref, 1 - parity, r)

    for k in range(TOP_K):
        pltpu.make_async_copy(ys_hbm.at[pl.ds(0, tp)], yg.at[slot, k], sems.at[slot]).wait()

    h2 = h_ref[...]
    for k in range(TOP_K):
        h2 = h2 + prob_ref[:, k:k + 1] * yg[slot, k]
    xn = (_rms(h2) * gin_ref[...]).astype(BF16)
    gate = jax.nn.sigmoid(jnp.dot(xn, wg_ref[...], preferred_element_type=F32))
    ple = _rms(jnp.dot(p_ref[...].astype(BF16), wp_ref[...], preferred_element_type=F32)) * gpost_ref[...]
    h3 = h2 + gate * ple
    if final:
        h3 = _rms(h3) * gfin_ref[...]
    o_ref[...] = h3


def _combine_ple(pos, probs, h1, ys, p, g_in, w_gate_bf16, w_proj_bf16, g_post, g_final, *, tp, final):
    n, d = h1.shape
    d_ple = p.shape[1]
    pos_tiles = _pos_tiles(pos, tp)
    whole = lambda a: pl.BlockSpec(a.shape, lambda i: (0,) * a.ndim)
    row_block = lambda width: pl.BlockSpec((tp, width), lambda i: (i, 0))
    return pl.pallas_call(
        functools.partial(_ple_body, final=final),
        grid=(n // tp,),
        in_specs=[
            pl.BlockSpec((None, 1, TOP_K * tp), lambda i: (i, 0, 0), memory_space=pltpu.SMEM),
            pl.BlockSpec((None, 1, TOP_K * tp), lambda i: (jnp.minimum(i + 1, n // tp - 1), 0, 0),
                         memory_space=pltpu.SMEM),
            row_block(TOP_K), row_block(d), pl.BlockSpec(memory_space=pl.ANY), row_block(d_ple),
            whole(g_in), whole(w_gate_bf16), whole(w_proj_bf16), whole(g_post), whole(g_final),
        ],
        out_specs=row_block(d),
        out_shape=jax.ShapeDtypeStruct((n, d), F32),
        scratch_shapes=[pltpu.VMEM((2, TOP_K, tp, d), F32), pltpu.SemaphoreType.DMA((2,))],
        compiler_params=_params(("arbitrary",)),
        name="combine_ple",
    )(pos_tiles, pos_tiles, probs, h1, ys, p, g_in, w_gate_bf16, w_proj_bf16, g_post, g_final)


def kernel(x, p, g_mix_norm, w_in, conv_w, w_pool, pool_scale, ln_v_g, ln_v_b, w_spatial, b_spatial, g_out,
           w_out, g_ffn_norm, w_router, b_router, w_gate, b_gate, w_up, b_up, w_down, b_down, g_ple_in,
           w_ple_gate, w_ple_proj, g_ple_post, g_final):
    batch, seq, d = x.shape
    depth = w_in.shape[0]
    n = batch * seq
    n_experts = w_router.shape[2]
    row = lambda v: v.reshape(1, -1)

    h = x.reshape(n, d)
    for i in range(depth):
        proj = _inproj(h, row(g_mix_norm[i]), w_in[i].astype(BF16), tm=512, tn=w_in.shape[2] // 2)
        y = _mixers(proj, conv_w[i], w_pool[i], row(pool_scale[i]), row(ln_v_g[i]), row(ln_v_b[i]),
                    w_spatial[i], b_spatial[i].T, row(g_out[i]), seq=seq, t_rows=256)
        w_router_pad = jnp.pad(w_router[i], ((0, 0), (0, ROUTER_LANES - n_experts)))
        w_router_hi = w_router_pad.astype(BF16)
        w_router_lo = (w_router_pad - w_router_hi.astype(F32)).astype(BF16)
        b_router_pad = jnp.pad(row(b_router[i]), ((0, 0), (0, ROUTER_LANES - n_experts)), constant_values=-jnp.inf)
        h1, xn, idx_pad, prob_pad = _outproj_router(y, h, w_out[i].astype(BF16), row(g_ffn_norm[i]),
                                                    w_router_hi, w_router_lo, b_router_pad, tm=256)
        top_idx, probs = idx_pad[:, :TOP_K], prob_pad[:, :TOP_K]
        plan = _routing_plan(top_idx, n_experts)
        xs = _dispatch(xn, plan, tp=256)
        ys = _moe_ffn(xs, plan, i, w_gate, b_gate, w_up, b_up, w_down, b_down)
        h = _combine_ple(plan["pos"], probs, h1, ys, p[i].reshape(n, -1), row(g_ple_in[i]),
                         w_ple_gate[i].astype(BF16), w_ple_proj[i].astype(BF16), row(g_ple_post[i]),
                         row(g_final), tp=256, final=(i == depth - 1))
    return h.reshape(batch, seq, d)
```

```python
import functools

import jax
import jax.numpy as jnp
from jax import lax
from jax.experimental import pallas as pl
from jax.experimental.pallas import tpu as pltpu

F32 = jnp.float32
BF16 = jnp.bfloat16

EPS = 1e-6
HEAD_DIM = 128
CHUNK = 128
CONV_WIDTH = 3
POOL_WINDOWS = (2, 4, 8, 16)
TOP_K = 4
SWIGLU_LIMIT = 7.0
SWIGLU_ALPHA = 1.702

HALO = 16
ROUTER_LANES = 128
VMEM_LIMIT = 56 * 1024 * 1024

SUB = 256
TILE_SUBS = 5
FF_BLOCK = 256
ISSUE_UNROLL = 8


def _rms(x):
    return x * lax.rsqrt(jnp.mean(x * x, axis=-1, keepdims=True) + EPS)


def _params(semantics):
    return pltpu.CompilerParams(dimension_semantics=semantics, vmem_limit_bytes=VMEM_LIMIT)


def _inproj_body(h_ref, g_ref, w_ref, o_ref):
    xn = _rms(h_ref[...]) * g_ref[...]
    o_ref[...] = jnp.dot(xn.astype(BF16), w_ref[...], preferred_element_type=F32).astype(o_ref.dtype)


def _inproj(h, g, w_bf16, *, tm, tn):
    n, d = h.shape
    d_in = w_bf16.shape[1]
    return pl.pallas_call(
        _inproj_body,
        grid=(d_in // tn, n // tm),
        in_specs=[
            pl.BlockSpec((tm, d), lambda j, i: (i, 0)),
            pl.BlockSpec((1, d), lambda j, i: (0, 0)),
            pl.BlockSpec((d, tn), lambda j, i: (0, j)),
        ],
        out_specs=pl.BlockSpec((tm, tn), lambda j, i: (i, j)),
        out_shape=jax.ShapeDtypeStruct((n, d_in), BF16),
        compiler_params=_params(("arbitrary", "arbitrary")),
        name="inproj",
    )(h, g, w_bf16)


def _mixer_body(proj_ref, halo_ref, convw_ref, wpool_ref, pscale_ref, lng_ref, lnb_ref,
                wsp_ref, bsp_ref, gout_ref, o_ref, za_scr, zb_scr, *, tiles_per_seq, d_conv, d_pool, d_sgu):
    t_rows = proj_ref.shape[0]
    i = pl.program_id(0)
    seq_tile = i % tiles_per_seq
    keep = (seq_tile > 0).astype(F32)

    o_za, o_ba, o_ca = 0, d_conv, 2 * d_conv
    o_zb = 3 * d_conv
    o_u = o_zb + d_pool
    o_v = o_u + d_sgu

    zc = proj_ref[:, o_ca:o_ca + d_conv].astype(F32) * proj_ref[:, o_za:o_za + d_conv].astype(F32)
    zc_halo = (halo_ref[:, o_ca:o_ca + d_conv].astype(F32) * halo_ref[:, o_za:o_za + d_conv].astype(F32)) * keep
    za_scr[0:HALO, :] = zc_halo
    za_scr[HALO:HALO + t_rows, :] = zc
    conv = zc * convw_ref[CONV_WIDTH - 1:CONV_WIDTH, :]
    for k in range(1, CONV_WIDTH):
        conv = conv + za_scr[HALO - k:HALO - k + t_rows, :] * convw_ref[CONV_WIDTH - 1 - k:CONV_WIDTH - k, :]
    y_a = proj_ref[:, o_ba:o_ba + d_conv].astype(F32) * conv

    zb = proj_ref[:, o_zb:o_zb + d_pool].astype(F32)
    zb_scr[0:HALO, :] = halo_ref[:, o_zb:o_zb + d_pool].astype(F32) * keep
    zb_scr[HALO:HALO + t_rows, :] = zb
    t_pos = seq_tile * t_rows + lax.broadcasted_iota(jnp.int32, (t_rows, 1), 0) + 1
    gdim = d_pool // len(POOL_WINDOWS)
    yb_parts = []
    for g, win in enumerate(POOL_WINDOWS):
        c0 = g * gdim
        s = zb[:, c0:c0 + gdim]
        for k in range(1, win):
            s = s + zb_scr[HALO - k:HALO - k + t_rows, c0:c0 + gdim]
        count = jnp.minimum(t_pos, win).astype(F32)
        pooled = s / count - zb[:, c0:c0 + gdim]
        yb_parts.append(jnp.dot(pooled.astype(BF16), wpool_ref[g].astype(BF16), preferred_element_type=F32))
    y_b = jnp.concatenate(yb_parts, axis=-1) * pscale_ref[...]

    v = proj_ref[:, o_v:o_v + d_sgu].astype(F32)
    mu = jnp.mean(v, axis=-1, keepdims=True)
    vc = v - mu
    var = jnp.mean(vc * vc, axis=-1, keepdims=True)
    vn = (vc * lax.rsqrt(var + EPS) * lng_ref[...] + lnb_ref[...]).astype(BF16)
    n_heads = d_sgu // HEAD_DIM
    row = lax.broadcasted_iota(jnp.int32, (CHUNK, CHUNK), 0)
    col = lax.broadcasted_iota(jnp.int32, (CHUNK, CHUNK), 1)
    yc_rows = []
    for c in range(t_rows // CHUNK):
        heads = []
        for hd in range(n_heads):
            ws = jnp.where(row >= col, wsp_ref[hd], 0.0).astype(BF16)
            vch = vn[c * CHUNK:(c + 1) * CHUNK, hd * HEAD_DIM:(hd + 1) * HEAD_DIM]
            heads.append(jnp.dot(ws, vch, preferred_element_type=F32) + bsp_ref[:, hd:hd + 1])
        yc_rows.append(jnp.concatenate(heads, axis=-1))
    y_c = proj_ref[:, o_u:o_u + d_sgu].astype(F32) * jnp.concatenate(yc_rows, axis=0)

    o_ref[:, 0:d_conv] = (_rms(y_a) * gout_ref[:, 0:d_conv]).astype(o_ref.dtype)
    o_ref[:, d_conv:d_conv + d_pool] = (_rms(y_b) * gout_ref[:, d_conv:d_conv + d_pool]).astype(o_ref.dtype)
    o_ref[:, d_conv + d_pool:] = (_rms(y_c) * gout_ref[:, d_conv + d_pool:]).astype(o_ref.dtype)


def _mixers(proj, conv_w, w_pool, pool_scale, ln_g, ln_b, w_spatial, b_spatial_t, g_out, *, seq, t_rows):
    n, d_in = proj.shape
    d_conv = conv_w.shape[1]
    d_pool = pool_scale.shape[1]
    d_sgu = ln_g.shape[1]
    d_mix = g_out.shape[1]
    halo_blocks = t_rows // HALO
    whole = lambda a: pl.BlockSpec(a.shape, lambda i: (0,) * a.ndim)
    body = functools.partial(_mixer_body, tiles_per_seq=seq // t_rows, d_conv=d_conv, d_pool=d_pool, d_sgu=d_sgu)
    return pl.pallas_call(
        body,
        grid=(n // t_rows,),
        in_specs=[
            pl.BlockSpec((t_rows, d_in), lambda i: (i, 0)),
            pl.BlockSpec((HALO, d_in), lambda i: (jnp.maximum(i * halo_blocks - 1, 0), 0)),
            whole(conv_w), whole(w_pool), whole(pool_scale), whole(ln_g), whole(ln_b),
            whole(w_spatial), whole(b_spatial_t), whole(g_out),
        ],
        out_specs=pl.BlockSpec((t_rows, d_mix), lambda i: (i, 0)),
        out_shape=jax.ShapeDtypeStruct((n, d_mix), BF16),
        scratch_shapes=[pltpu.VMEM((HALO + t_rows, d_conv), F32), pltpu.VMEM((HALO + t_rows, d_pool), F32)],
        compiler_params=_params(("arbitrary",)),
        name="mixers",
    )(proj, proj, conv_w, w_pool, pool_scale, ln_g, ln_b, w_spatial, b_spatial_t, g_out)


def _outproj_router_body(y_ref, h_ref, w_ref, g_ref, wrh_ref, wrl_ref, br_ref, h1_ref, xn_ref, idx_ref, prob_ref,
                         rank_ref, count_ref, *, chain):
    @pl.when(pl.program_id(0) == 0)
    def _():
        count_ref[...] = jnp.zeros(count_ref.shape, count_ref.dtype)

    lane = lax.broadcasted_iota(jnp.int32, (chain, ROUTER_LANES), 1)
    earlier = (lax.broadcasted_iota(jnp.int32, (chain, chain), 0)
               > lax.broadcasted_iota(jnp.int32, (chain, chain), 1)).astype(BF16)
    for c in range(h_ref.shape[0] // chain):
        rows = slice(c * chain, (c + 1) * chain)
        h1 = h_ref[rows, :] + jnp.dot(y_ref[rows, :], w_ref[...], preferred_element_type=F32)
        h1_ref[rows, :] = h1
        xn = _rms(h1) * g_ref[...]
        xn_ref[rows, :] = xn
        x_hi = xn.astype(BF16)
        x_lo = (xn - x_hi.astype(F32)).astype(BF16)
        logits = (jnp.dot(x_hi, wrh_ref[...], preferred_element_type=F32)
                  + (jnp.dot(x_lo, wrh_ref[...], preferred_element_type=F32)
                     + jnp.dot(x_hi, wrl_ref[...], preferred_element_type=F32))) + br_ref[...]
        vals, idxs = [], []
        for _ in range(TOP_K):
            m = jnp.max(logits, axis=-1, keepdims=True)
            sel = jnp.min(jnp.where(logits == m, lane, ROUTER_LANES), axis=-1, keepdims=True)
            vals.append(m)
            idxs.append(sel)
            logits = jnp.where(lane == sel, -jnp.inf, logits)
        exps = [jnp.exp(v - vals[0]) for v in vals]
        denom = exps[0]
        for e in exps[1:]:
            denom = denom + e
        chosen = jnp.zeros(lane.shape, F32)
        for k in range(TOP_K):
            chosen = chosen + (lane == idxs[k]).astype(F32)
        before = count_ref[...] + jnp.dot(earlier, chosen.astype(BF16), preferred_element_type=F32)
        count_ref[...] = count_ref[...] + jnp.sum(chosen, axis=0, keepdims=True)
        idx_out = jnp.zeros(lane.shape, jnp.int32)
        prob_out = jnp.zeros(lane.shape, F32)
        rank_out = jnp.zeros(lane.shape, jnp.int32)
        for k in range(TOP_K):
            rank_k = jnp.sum(jnp.where(lane == idxs[k], before, 0.0), axis=-1, keepdims=True).astype(jnp.int32)
            idx_out = jnp.where(lane == k, idxs[k], idx_out)
            prob_out = jnp.where(lane == k, exps[k] / denom, prob_out)
            rank_out = jnp.where(lane == k, rank_k, rank_out)
        idx_ref[rows, :] = idx_out
        prob_ref[rows, :] = prob_out
        rank_ref[rows, :] = rank_out


def _outproj_router(y, h, w_out_bf16, g_ffn, w_router_hi, w_router_lo, b_router_pad, *, tm, chain):
    n, d = h.shape
    whole = lambda a: pl.BlockSpec(a.shape, lambda i: (0,) * a.ndim)
    row_block = lambda width: pl.BlockSpec((tm, width), lambda i: (i, 0))
    lanes = lambda dtype: jax.ShapeDtypeStruct((n, ROUTER_LANES), dtype)
    return pl.pallas_call(
        functools.partial(_outproj_router_body, chain=chain),
        grid=(n // tm,),
        in_specs=[row_block(y.shape[1]), row_block(d),
                  pl.BlockSpec(w_out_bf16.shape, lambda i: (0, 0), pipeline_mode=pl.Buffered(1)), whole(g_ffn),
                  whole(w_router_hi), whole(w_router_lo), whole(b_router_pad)],
        out_specs=[row_block(d), row_block(d), row_block(ROUTER_LANES), row_block(ROUTER_LANES),
                   row_block(ROUTER_LANES), pl.BlockSpec((1, ROUTER_LANES), lambda i: (0, 0))],
        out_shape=[jax.ShapeDtypeStruct((n, d), F32), jax.ShapeDtypeStruct((n, d), F32),
                   lanes(jnp.int32), lanes(F32), lanes(jnp.int32),
                   jax.ShapeDtypeStruct((1, ROUTER_LANES), F32)],
        compiler_params=_params(("arbitrary",)),
        name="outproj_router",
    )(y, h, w_out_bf16, g_ffn, w_router_hi, w_router_lo, b_router_pad)


def _routing_plan(top_idx, rank, counts):
    n = top_idx.shape[0]
    n_experts = counts.shape[0]
    max_subs = (n * TOP_K) // SUB + n_experts
    max_tiles = max_subs // TILE_SUBS + n_experts
    nsub = (counts + SUB - 1) // SUB
    sub_end = jnp.cumsum(nsub)
    sub_off = sub_end - nsub
    is_expert = top_idx[:, :, None] == jnp.arange(n_experts, dtype=jnp.int32)
    pos = jnp.sum(jnp.where(is_expert, sub_off * SUB, 0), axis=-1) + rank
    block = jnp.arange(max_subs, dtype=jnp.int32)
    is_last = ((block[:, None] == (sub_end - 1)[None, :]) & (nsub > 0)[None, :]).any(axis=1)
    pad_block = (is_last | (block >= sub_end[-1])).astype(jnp.int32)
    ntile = (nsub + TILE_SUBS - 1) // TILE_SUBS
    tile_end = jnp.cumsum(ntile)
    tile_off = tile_end - ntile
    n_tiles = tile_end[-1]
    t = jnp.arange(max_tiles, dtype=jnp.int32)
    t_eff = jnp.minimum(t, n_tiles - 1)
    expert = jnp.sum((tile_end[None, :] <= t_eff[:, None]).astype(jnp.int32), axis=1)
    within = t_eff - tile_off[expert]
    start_sub = sub_off[expert] + within * TILE_SUBS
    tile_nsub = jnp.where(t < n_tiles, jnp.minimum(TILE_SUBS, nsub[expert] - within * TILE_SUBS), 0)
    return dict(pos=pos, tile_expert=expert, tile_nsub=tile_nsub.astype(jnp.int32),
                tile_start=start_sub.astype(jnp.int32), used_subs=sub_end[-1:].astype(jnp.int32),
                pad_block=pad_block,
                n_pad_blocks=jnp.sum(pad_block).reshape(1), n_rows=max_subs * SUB)


def _pos_tiles(pos, tp):
    n = pos.shape[0]
    return pos.reshape(n // tp, tp, TOP_K).transpose(0, 2, 1).reshape(n // tp, 1, TOP_K * tp)


def _dispatch_body(pad_ref, npad_ref, pos_ref, x_ref, xs_hbm, zeros, sem):
    i = pl.program_id(0)
    tp, d = x_ref.shape

    def block_copy(b):
        return pltpu.make_async_copy(zeros, xs_hbm.at[pl.ds(pl.multiple_of(b * SUB, SUB), SUB)], sem)

    @pl.when(i == 0)
    def _():
        zeros[...] = jnp.zeros(zeros.shape, zeros.dtype)

        def push(b, carry):
            @pl.when(pad_ref[b] > 0)
            def _():
                block_copy(b).start()
            return carry
        lax.fori_loop(0, pad_ref.shape[0], push, 0)

        def drain(b, carry):
            block_copy(0).wait()
            return carry
        lax.fori_loop(0, npad_ref[0], drain, 0)

    def issue(c, carry):
        for u in range(ISSUE_UNROLL):
            r = c * ISSUE_UNROLL + u
            for k in range(TOP_K):
                pltpu.make_async_copy(x_ref.at[pl.ds(r, 1)], xs_hbm.at[pl.ds(pos_ref[0, k * tp + r], 1)],
                                      sem).start(priority=k % 2)
        return carry
    lax.fori_loop(0, tp // ISSUE_UNROLL, issue, 0)
    for k in range(TOP_K):
        pltpu.make_async_copy(x_ref, xs_hbm.at[pl.ds(0, tp)], sem).wait()


def _dispatch(xn, plan, *, tp):
    n, d = xn.shape
    grid_spec = pltpu.PrefetchScalarGridSpec(
        num_scalar_prefetch=2,
        grid=(n // tp,),
        in_specs=[
            pl.BlockSpec((None, 1, TOP_K * tp), lambda i, pad, npad: (i, 0, 0), memory_space=pltpu.SMEM),
            pl.BlockSpec((tp, d), lambda i, pad, npad: (i, 0)),
        ],
        out_specs=pl.BlockSpec(memory_space=pl.ANY),
        scratch_shapes=[pltpu.VMEM((SUB, d), xn.dtype), pltpu.SemaphoreType.DMA],
    )
    return pl.pallas_call(
        _dispatch_body,
        grid_spec=grid_spec,
        out_shape=jax.ShapeDtypeStruct((plan["n_rows"], d), xn.dtype),
        compiler_params=_params(("arbitrary",)),
        name="dispatch",
    )(plan["pad_block"], plan["n_pad_blocks"], _pos_tiles(plan["pos"], tp), xn)


def _row_copy(src_hbm, dst_vmem, sem, src_row, dst_row):
    return pltpu.make_async_copy(src_hbm.at[pl.ds(src_row, 1)], dst_vmem.at[pl.ds(dst_row, 1)], sem)


def _moe_body(expert_ref, nsub_ref, start_ref, used_ref, xs_hbm, wg_ref, bg_ref, wu_ref, bu_ref, wd_ref, bd_ref,
              ys_hbm, xg, xb, acc, wgb, wub, wdb, gsem, osem, *, n_t, n_f):
    del expert_ref
    t = pl.program_id(0)
    f = pl.program_id(1)
    ns = nsub_ref[t]
    t_next = jnp.minimum(t + 1, n_t - 1)
    ns_next = jnp.where(t + 1 < n_t, nsub_ref[t_next], 0)

    def sub_rows(j):
        return pl.ds(pl.multiple_of(j * SUB, SUB), SUB)

    def in_copy(tile, j):
        src = pl.ds(pl.multiple_of((start_ref[tile] + j) * SUB, SUB), SUB)
        return pltpu.make_async_copy(xs_hbm.at[src], xg.at[sub_rows(j)], gsem)

    def out_copy(tile, j):
        dst = pl.ds(pl.multiple_of((start_ref[tile] + j) * SUB, SUB), SUB)
        return pltpu.make_async_copy(acc.at[sub_rows(j)], ys_hbm.at[dst], osem)

    @pl.when((t == 0) & (f == 0))
    def _():
        used = used_ref[0]
        n_tail = ys_hbm.shape[0] // SUB - used
        xg[0:SUB, :] = jnp.zeros((SUB, xg.shape[1]), F32)

        def tail_copy(j):
            dst = pl.ds(pl.multiple_of((used + j) * SUB, SUB), SUB)
            return pltpu.make_async_copy(xg.at[pl.ds(0, SUB)], ys_hbm.at[dst], osem)

        def push(j, carry):
            tail_copy(j).start()
            return carry
        lax.fori_loop(0, n_tail, push, 0)

        def drain(j, carry):
            tail_copy(j).wait()
            return carry
        lax.fori_loop(0, n_tail, drain, 0)

    @pl.when((f == 0) & (t > 0))
    def _():
        def drain(j, carry):
            out_copy(t - 1, j).wait()
            return carry
        lax.fori_loop(0, nsub_ref[jnp.maximum(t - 1, 0)], drain, 0)

    @pl.when(ns > 0)
    def _():
        @pl.when(f == 0)
        def _():
            @pl.when(t == 0)
            def _():
                def fetch(j, carry):
                    in_copy(t, j).start()
                    return carry
                lax.fori_loop(0, ns, fetch, 0)

            def land(j, carry):
                in_copy(t, j).wait()
                return carry
            lax.fori_loop(0, ns, land, 0)

            def narrow(j, carry):
                xb[sub_rows(j), :] = xg[sub_rows(j), :].astype(BF16)
                acc[sub_rows(j), :] = jnp.broadcast_to(bd_ref[...], (SUB, acc.shape[1]))
                return carry
            lax.fori_loop(0, ns, narrow, 0)

            def prefetch(j, carry):
                in_copy(t_next, j).start()
                return carry
            lax.fori_loop(0, ns_next, prefetch, 0)

        wgb[...] = wg_ref[...].astype(BF16)
        wub[...] = wu_ref[...].astype(BF16)
        wdb[...] = wd_ref[...].astype(BF16)

        def ffn(rows):
            x = xb[rows, :]
            g = jnp.minimum(jnp.dot(x, wgb[...], preferred_element_type=F32) + bg_ref[...], SWIGLU_LIMIT)
            u = jnp.clip(jnp.dot(x, wub[...], preferred_element_type=F32) + bu_ref[...], -SWIGLU_LIMIT, SWIGLU_LIMIT)
            a = (u + 1.0) * g * jax.nn.sigmoid(SWIGLU_ALPHA * g)
            acc[rows, :] += jnp.dot(a.astype(BF16), wdb[...], preferred_element_type=F32)

        def ffn_quad(q, carry):
            ffn(pl.ds(pl.multiple_of(q * (4 * SUB), 4 * SUB), 4 * SUB))
            return carry
        n_quad = lax.shift_right_logical(ns, 2)
        lax.fori_loop(0, n_quad, ffn_quad, 0)

        @pl.when((ns & 2) == 2)
        def _():
            ffn(pl.ds(pl.multiple_of(n_quad * (4 * SUB), 2 * SUB), 2 * SUB))

        @pl.when((ns & 1) == 1)
        def _():
            ffn(sub_rows(ns - 1))

        @pl.when(f == n_f - 1)
        def _():
            def push(j, carry):
                out_copy(t, j).start()
                return carry
            lax.fori_loop(0, ns, push, 0)

            @pl.when(t == n_t - 1)
            def _():
                def drain(j, carry):
                    out_copy(t, j).wait()
                    return carry
                lax.fori_loop(0, ns, drain, 0)


def _moe_ffn(xs, plan, layer, w_gate, b_gate, w_up, b_up, w_down, b_down):
    d = xs.shape[1]
    depth, n_experts, _, d_ff = w_gate.shape
    tile_rows = SUB * TILE_SUBS
    max_tiles = plan["tile_expert"].shape[0]
    n_f = d_ff // FF_BLOCK

    def ff_block(f, nsub_ref, t):
        return jnp.where(nsub_ref[t] > 0, f, n_f - 1)

    col_w = pl.BlockSpec((None, None, d, FF_BLOCK),
                         lambda t, f, e, ns, st, used: (layer, e[t], 0, ff_block(f, ns, t)))
    col_b = pl.BlockSpec((None, None, 1, FF_BLOCK),
                         lambda t, f, e, ns, st, used: (layer, e[t], 0, ff_block(f, ns, t)))
    grid_spec = pltpu.PrefetchScalarGridSpec(
        num_scalar_prefetch=4,
        grid=(max_tiles, n_f),
        in_specs=[
            pl.BlockSpec(memory_space=pl.ANY),
            col_w, col_b, col_w, col_b,
            pl.BlockSpec((None, None, FF_BLOCK, d),
                         lambda t, f, e, ns, st, used: (layer, e[t], ff_block(f, ns, t), 0)),
            pl.BlockSpec((None, None, 1, d), lambda t, f, e, ns, st, used: (layer, e[t], 0, 0)),
        ],
        out_specs=pl.BlockSpec(memory_space=pl.ANY),
        scratch_shapes=[
            pltpu.VMEM((tile_rows, d), F32), pltpu.VMEM((tile_rows, d), BF16), pltpu.VMEM((tile_rows, d), F32),
            pltpu.VMEM((d, FF_BLOCK), BF16), pltpu.VMEM((d, FF_BLOCK), BF16), pltpu.VMEM((FF_BLOCK, d), BF16),
            pltpu.SemaphoreType.DMA, pltpu.SemaphoreType.DMA,
        ],
    )
    return pl.pallas_call(
        functools.partial(_moe_body, n_t=max_tiles, n_f=n_f),
        grid_spec=grid_spec,
        out_shape=jax.ShapeDtypeStruct((plan["n_rows"], d), F32),
        compiler_params=_params(("arbitrary", "arbitrary")),
        name="moe_ffn",
    )(plan["tile_expert"], plan["tile_nsub"], plan["tile_start"], plan["used_subs"], xs,
      w_gate, b_gate.reshape(depth, n_experts, 1, d_ff), w_up, b_up.reshape(depth, n_experts, 1, d_ff),
      w_down, b_down.reshape(depth, n_experts, 1, d))


def _ple_body(pos_ref, next_pos_ref, prob_ref, h_ref, ys_hbm, p_ref, gin_ref, wg_ref, wp_ref, gpost_ref, gfin_ref,
              o_ref, yg, sems, *, final):
    tp = h_ref.shape[0]
    i = pl.program_id(0)
    n_i = pl.num_programs(0)
    slot = i & 1

    def fetch_row(idx_ref, dst_slot, r):
        for k in range(TOP_K):
            _row_copy(ys_hbm, yg.at[dst_slot, k], sems.at[dst_slot], idx_ref[0, k * tp + r], r).start(priority=k % 2)

    @pl.when(i == 0)
    def _():
        def issue(r, carry):
            fetch_row(pos_ref, 0, r)
            return carry
        lax.fori_loop(0, tp, issue, 0, unroll=2)

    for parity in range(2):
        @pl.when((i + 1 < n_i) & (slot == parity))
        def _():
            for r in range(tp):
                fetch_row(next_pos_ref, 1 - parity, r)

    for k in range(TOP_K):
        pltpu.make_async_copy(ys_hbm.at[pl.ds(0, tp)], yg.at[slot, k], sems.at[slot]).wait()

    h2 = h_ref[...]
    for k in range(TOP_K):
        h2 = h2 + prob_ref[:, k:k + 1] * yg[slot, k]
    xn = (_rms(h2) * gin_ref[...]).astype(BF16)
    gate = jax.nn.sigmoid(jnp.dot(xn, wg_ref[...], preferred_element_type=F32))
    ple = _rms(jnp.dot(p_ref[...].astype(BF16), wp_ref[...], preferred_element_type=F32)) * gpost_ref[...]
    h3 = h2 + gate * ple
    if final:
        h3 = _rms(h3) * gfin_ref[...]
    o_ref[...] = h3


def _combine_ple(pos, probs, h1, ys, p, g_in, w_gate_bf16, w_proj_bf16, g_post, g_final, *, tp, final):
    n, d = h1.shape
    d_ple = p.shape[1]
    pos_tiles = _pos_tiles(pos, tp)
    whole = lambda a: pl.BlockSpec(a.shape, lambda i: (0,) * a.ndim)
    row_block = lambda width: pl.BlockSpec((tp, width), lambda i: (i, 0))
    return pl.pallas_call(
        functools.partial(_ple_body, final=final),
        grid=(n // tp,),
        in_specs=[
            pl.BlockSpec((None, 1, TOP_K * tp), lambda i: (i, 0, 0), memory_space=pltpu.SMEM),
            pl.BlockSpec((None, 1, TOP_K * tp), lambda i: (jnp.minimum(i + 1, n // tp - 1), 0, 0),
                         memory_space=pltpu.SMEM),
            row_block(TOP_K), row_block(d), pl.BlockSpec(memory_space=pl.ANY), row_block(d_ple),
            whole(g_in), whole(w_gate_bf16), whole(w_proj_bf16), whole(g_post), whole(g_final),
        ],
        out_specs=row_block(d),
        out_shape=jax.ShapeDtypeStruct((n, d), F32),
        scratch_shapes=[pltpu.VMEM((2, TOP_K, tp, d), F32), pltpu.SemaphoreType.DMA((2,))],
        compiler_params=_params(("arbitrary",)),
        name="combine_ple",
    )(pos_tiles, pos_tiles, probs, h1, ys, p, g_in, w_gate_bf16, w_proj_bf16, g_post, g_final)


def kernel(x, p, g_mix_norm, w_in, conv_w, w_pool, pool_scale, ln_v_g, ln_v_b, w_spatial, b_spatial, g_out,
           w_out, g_ffn_norm, w_router, b_router, w_gate, b_gate, w_up, b_up, w_down, b_down, g_ple_in,
           w_ple_gate, w_ple_proj, g_ple_post, g_final):
    batch, seq, d = x.shape
    depth = w_in.shape[0]
    n = batch * seq
    n_experts = w_router.shape[2]
    row = lambda v: v.reshape(1, -1)

    h = x.reshape(n, d)
    for i in range(depth):
        proj = _inproj(h, row(g_mix_norm[i]), w_in[i].astype(BF16), tm=512, tn=w_in.shape[2] // 2)
        y = _mixers(proj, conv_w[i], w_pool[i], row(pool_scale[i]), row(ln_v_g[i]), row(ln_v_b[i]),
                    w_spatial[i], b_spatial[i].T, row(g_out[i]), seq=seq, t_rows=256)
        w_router_pad = jnp.pad(w_router[i], ((0, 0), (0, ROUTER_LANES - n_experts)))
        w_router_hi = w_router_pad.astype(BF16)
        w_router_lo = (w_router_pad - w_router_hi.astype(F32)).astype(BF16)
        b_router_pad = jnp.pad(row(b_router[i]), ((0, 0), (0, ROUTER_LANES - n_experts)), constant_values=-jnp.inf)
        h1, xn, idx_pad, prob_pad, rank_pad, count_pad = _outproj_router(
            y, h, w_out[i].astype(BF16), row(g_ffn_norm[i]), w_router_hi, w_router_lo, b_router_pad, tm=512, chain=512)
        top_idx, probs = idx_pad[:, :TOP_K], prob_pad[:, :TOP_K]
        plan = _routing_plan(top_idx, rank_pad[:, :TOP_K], count_pad[0, :n_experts].astype(jnp.int32))
        xs = _dispatch(xn, plan, tp=256)
        ys = _moe_ffn(xs, plan, i, w_gate, b_gate, w_up, b_up, w_down, b_down)
        h = _combine_ple(plan["pos"], probs, h1, ys, p[i].reshape(n, -1), row(g_ple_in[i]),
                         w_ple_gate[i].astype(BF16), w_ple_proj[i].astype(BF16), row(g_ple_post[i]),
                         row(g_final), tp=256, final=(i == depth - 1))
    return h.reshape(batch, seq, d)
```

```python
import functools

import jax
import jax.numpy as jnp
from jax import lax
from jax.experimental import pallas as pl
from jax.experimental.pallas import tpu as pltpu

F32 = jnp.float32
BF16 = jnp.bfloat16

EPS = 1e-6
HEAD_DIM = 128
CHUNK = 128
CONV_WIDTH = 3
POOL_WINDOWS = (2, 4, 8, 16)
TOP_K = 4
SWIGLU_LIMIT = 7.0
SWIGLU_ALPHA = 1.702

HALO = 16
ROUTER_LANES = 128
VMEM_LIMIT = 56 * 1024 * 1024

SUB = 128
TILE_SUBS = 10
FF_BLOCK = 256
FFN_CHUNKS = (8, 4, 2, 1)
ISSUE_UNROLL = 8


def _rms(x):
    return x * lax.rsqrt(jnp.mean(x * x, axis=-1, keepdims=True) + EPS)


def _params(semantics):
    return pltpu.CompilerParams(dimension_semantics=semantics, vmem_limit_bytes=VMEM_LIMIT)


def _inproj_body(h_ref, g_ref, w_ref, o_ref):
    xn = _rms(h_ref[...]) * g_ref[...]
    o_ref[...] = jnp.dot(xn.astype(BF16), w_ref[...], preferred_element_type=F32).astype(o_ref.dtype)


def _inproj(h, g, w_bf16, *, tm, tn):
    n, d = h.shape
    d_in = w_bf16.shape[1]
    return pl.pallas_call(
        _inproj_body,
        grid=(d_in // tn, n // tm),
        in_specs=[
            pl.BlockSpec((tm, d), lambda j, i: (i, 0)),
            pl.BlockSpec((1, d), lambda j, i: (0, 0)),
            pl.BlockSpec((d, tn), lambda j, i: (0, j)),
        ],
        out_specs=pl.BlockSpec((tm, tn), lambda j, i: (i, j)),
        out_shape=jax.ShapeDtypeStruct((n, d_in), BF16),
        compiler_params=_params(("arbitrary", "arbitrary")),
        name="inproj",
    )(h, g, w_bf16)


def _mixer_body(proj_ref, halo_ref, convw_ref, wpool_ref, pscale_ref, lng_ref, lnb_ref,
                wsp_ref, bsp_ref, gout_ref, o_ref, za_scr, zb_scr, *, tiles_per_seq, d_conv, d_pool, d_sgu):
    t_rows = proj_ref.shape[0]
    i = pl.program_id(0)
    seq_tile = i % tiles_per_seq
    keep = (seq_tile > 0).astype(F32)

    o_za, o_ba, o_ca = 0, d_conv, 2 * d_conv
    o_zb = 3 * d_conv
    o_u = o_zb + d_pool
    o_v = o_u + d_sgu

    zc = proj_ref[:, o_ca:o_ca + d_conv].astype(F32) * proj_ref[:, o_za:o_za + d_conv].astype(F32)
    zc_halo = (halo_ref[:, o_ca:o_ca + d_conv].astype(F32) * halo_ref[:, o_za:o_za + d_conv].astype(F32)) * keep
    za_scr[0:HALO, :] = zc_halo
    za_scr[HALO:HALO + t_rows, :] = zc
    conv = zc * convw_ref[CONV_WIDTH - 1:CONV_WIDTH, :]
    for k in range(1, CONV_WIDTH):
        conv = conv + za_scr[HALO - k:HALO - k + t_rows, :] * convw_ref[CONV_WIDTH - 1 - k:CONV_WIDTH - k, :]
    y_a = proj_ref[:, o_ba:o_ba + d_conv].astype(F32) * conv

    zb = proj_ref[:, o_zb:o_zb + d_pool].astype(F32)
    zb_scr[0:HALO, :] = halo_ref[:, o_zb:o_zb + d_pool].astype(F32) * keep
    zb_scr[HALO:HALO + t_rows, :] = zb
    t_pos = seq_tile * t_rows + lax.broadcasted_iota(jnp.int32, (t_rows, 1), 0) + 1
    gdim = d_pool // len(POOL_WINDOWS)
    yb_parts = []
    for g, win in enumerate(POOL_WINDOWS):
        c0 = g * gdim
        s = zb[:, c0:c0 + gdim]
        for k in range(1, win):
            s = s + zb_scr[HALO - k:HALO - k + t_rows, c0:c0 + gdim]
        count = jnp.minimum(t_pos, win).astype(F32)
        pooled = s / count - zb[:, c0:c0 + gdim]
        yb_parts.append(jnp.dot(pooled.astype(BF16), wpool_ref[g].astype(BF16), preferred_element_type=F32))
    y_b = jnp.concatenate(yb_parts, axis=-1) * pscale_ref[...]

    v = proj_ref[:, o_v:o_v + d_sgu].astype(F32)
    mu = jnp.mean(v, axis=-1, keepdims=True)
    vc = v - mu
    var = jnp.mean(vc * vc, axis=-1, keepdims=True)
    vn = (vc * lax.rsqrt(var + EPS) * lng_ref[...] + lnb_ref[...]).astype(BF16)
    n_heads = d_sgu // HEAD_DIM
    row = lax.broadcasted_iota(jnp.int32, (CHUNK, CHUNK), 0)
    col = lax.broadcasted_iota(jnp.int32, (CHUNK, CHUNK), 1)
    yc_rows = []
    for c in range(t_rows // CHUNK):
        heads = []
        for hd in range(n_heads):
            ws = jnp.where(row >= col, wsp_ref[hd], 0.0).astype(BF16)
            vch = vn[c * CHUNK:(c + 1) * CHUNK, hd * HEAD_DIM:(hd + 1) * HEAD_DIM]
            heads.append(jnp.dot(ws, vch, preferred_element_type=F32) + bsp_ref[:, hd:hd + 1])
        yc_rows.append(jnp.concatenate(heads, axis=-1))
    y_c = proj_ref[:, o_u:o_u + d_sgu].astype(F32) * jnp.concatenate(yc_rows, axis=0)

    o_ref[:, 0:d_conv] = (_rms(y_a) * gout_ref[:, 0:d_conv]).astype(o_ref.dtype)
    o_ref[:, d_conv:d_conv + d_pool] = (_rms(y_b) * gout_ref[:, d_conv:d_conv + d_pool]).astype(o_ref.dtype)
    o_ref[:, d_conv + d_pool:] = (_rms(y_c) * gout_ref[:, d_conv + d_pool:]).astype(o_ref.dtype)


def _mixers(proj, conv_w, w_pool, pool_scale, ln_g, ln_b, w_spatial, b_spatial_t, g_out, *, seq, t_rows):
    n, d_in = proj.shape
    d_conv = conv_w.shape[1]
    d_pool = pool_scale.shape[1]
    d_sgu = ln_g.shape[1]
    d_mix = g_out.shape[1]
    halo_blocks = t_rows // HALO
    whole = lambda a: pl.BlockSpec(a.shape, lambda i: (0,) * a.ndim)
    body = functools.partial(_mixer_body, tiles_per_seq=seq // t_rows, d_conv=d_conv, d_pool=d_pool, d_sgu=d_sgu)
    return pl.pallas_call(
        body,
        grid=(n // t_rows,),
        in_specs=[
            pl.BlockSpec((t_rows, d_in), lambda i: (i, 0)),
            pl.BlockSpec((HALO, d_in), lambda i: (jnp.maximum(i * halo_blocks - 1, 0), 0)),
            whole(conv_w), whole(w_pool), whole(pool_scale), whole(ln_g), whole(ln_b),
            whole(w_spatial), whole(b_spatial_t), whole(g_out),
        ],
        out_specs=pl.BlockSpec((t_rows, d_mix), lambda i: (i, 0)),
        out_shape=jax.ShapeDtypeStruct((n, d_mix), BF16),
        scratch_shapes=[pltpu.VMEM((HALO + t_rows, d_conv), F32), pltpu.VMEM((HALO + t_rows, d_pool), F32)],
        compiler_params=_params(("arbitrary",)),
        name="mixers",
    )(proj, proj, conv_w, w_pool, pool_scale, ln_g, ln_b, w_spatial, b_spatial_t, g_out)


def _outproj_router_body(y_ref, h_ref, w_ref, g_ref, wrh_ref, wrl_ref, br_ref, h1_ref, xn_ref, idx_ref, prob_ref,
                         rank_ref, count_ref, *, chain):
    @pl.when(pl.program_id(0) == 0)
    def _():
        count_ref[...] = jnp.zeros(count_ref.shape, count_ref.dtype)

    lane = lax.broadcasted_iota(jnp.int32, (chain, ROUTER_LANES), 1)
    earlier = (lax.broadcasted_iota(jnp.int32, (chain, chain), 0)
               > lax.broadcasted_iota(jnp.int32, (chain, chain), 1)).astype(BF16)
    for c in range(h_ref.shape[0] // chain):
        rows = slice(c * chain, (c + 1) * chain)
        h1 = h_ref[rows, :] + jnp.dot(y_ref[rows, :], w_ref[...], preferred_element_type=F32)
        h1_ref[rows, :] = h1
        xn = _rms(h1) * g_ref[...]
        xn_ref[rows, :] = xn
        x_hi = xn.astype(BF16)
        x_lo = (xn - x_hi.astype(F32)).astype(BF16)
        logits = (jnp.dot(x_hi, wrh_ref[...], preferred_element_type=F32)
                  + (jnp.dot(x_lo, wrh_ref[...], preferred_element_type=F32)
                     + jnp.dot(x_hi, wrl_ref[...], preferred_element_type=F32))) + br_ref[...]
        vals, idxs = [], []
        for _ in range(TOP_K):
            m = jnp.max(logits, axis=-1, keepdims=True)
            sel = jnp.min(jnp.where(logits == m, lane, ROUTER_LANES), axis=-1, keepdims=True)
            vals.append(m)
            idxs.append(sel)
            logits = jnp.where(lane == sel, -jnp.inf, logits)
        exps = [jnp.exp(v - vals[0]) for v in vals]
        denom = exps[0]
        for e in exps[1:]:
            denom = denom + e
        chosen = jnp.zeros(lane.shape, F32)
        for k in range(TOP_K):
            chosen = chosen + (lane == idxs[k]).astype(F32)
        before = count_ref[...] + jnp.dot(earlier, chosen.astype(BF16), preferred_element_type=F32)
        count_ref[...] = count_ref[...] + jnp.sum(chosen, axis=0, keepdims=True)
        idx_out = jnp.zeros(lane.shape, jnp.int32)
        prob_out = jnp.zeros(lane.shape, F32)
        rank_out = jnp.zeros(lane.shape, jnp.int32)
        for k in range(TOP_K):
            rank_k = jnp.sum(jnp.where(lane == idxs[k], before, 0.0), axis=-1, keepdims=True).astype(jnp.int32)
            idx_out = jnp.where(lane == k, idxs[k], idx_out)
            prob_out = jnp.where(lane == k, exps[k] / denom, prob_out)
            rank_out = jnp.where(lane == k, rank_k, rank_out)
        idx_ref[rows, :] = idx_out
        prob_ref[rows, :] = prob_out
        rank_ref[rows, :] = rank_out


def _outproj_router(y, h, w_out_bf16, g_ffn, w_router_hi, w_router_lo, b_router_pad, *, tm, chain):
    n, d = h.shape
    whole = lambda a: pl.BlockSpec(a.shape, lambda i: (0,) * a.ndim)
    row_block = lambda width: pl.BlockSpec((tm, width), lambda i: (i, 0))
    lanes = lambda dtype: jax.ShapeDtypeStruct((n, ROUTER_LANES), dtype)
    return pl.pallas_call(
        functools.partial(_outproj_router_body, chain=chain),
        grid=(n // tm,),
        in_specs=[row_block(y.shape[1]), row_block(d),
                  pl.BlockSpec(w_out_bf16.shape, lambda i: (0, 0), pipeline_mode=pl.Buffered(1)), whole(g_ffn),
                  whole(w_router_hi), whole(w_router_lo), whole(b_router_pad)],
        out_specs=[row_block(d), row_block(d), row_block(ROUTER_LANES), row_block(ROUTER_LANES),
                   row_block(ROUTER_LANES), pl.BlockSpec((1, ROUTER_LANES), lambda i: (0, 0))],
        out_shape=[jax.ShapeDtypeStruct((n, d), F32), jax.ShapeDtypeStruct((n, d), F32),
                   lanes(jnp.int32), lanes(F32), lanes(jnp.int32),
                   jax.ShapeDtypeStruct((1, ROUTER_LANES), F32)],
        compiler_params=_params(("arbitrary",)),
        name="outproj_router",
    )(y, h, w_out_bf16, g_ffn, w_router_hi, w_router_lo, b_router_pad)


def _routing_plan(top_idx, rank, counts):
    n = top_idx.shape[0]
    n_experts = counts.shape[0]
    max_subs = (n * TOP_K) // SUB + n_experts
    max_tiles = max_subs // TILE_SUBS + n_experts
    nsub = (counts + SUB - 1) // SUB
    sub_end = jnp.cumsum(nsub)
    sub_off = sub_end - nsub
    is_expert = top_idx[:, :, None] == jnp.arange(n_experts, dtype=jnp.int32)
    pos = jnp.sum(jnp.where(is_expert, sub_off * SUB, 0), axis=-1) + rank
    block = jnp.arange(max_subs, dtype=jnp.int32)
    is_last = ((block[:, None] == (sub_end - 1)[None, :]) & (nsub > 0)[None, :]).any(axis=1)
    pad_block = (is_last | (block >= sub_end[-1])).astype(jnp.int32)
    ntile = (nsub + TILE_SUBS - 1) // TILE_SUBS
    tile_end = jnp.cumsum(ntile)
    tile_off = tile_end - ntile
    n_tiles = tile_end[-1]
    t = jnp.arange(max_tiles, dtype=jnp.int32)
    t_eff = jnp.minimum(t, n_tiles - 1)
    expert = jnp.sum((tile_end[None, :] <= t_eff[:, None]).astype(jnp.int32), axis=1)
    within = t_eff - tile_off[expert]
    start_sub = sub_off[expert] + within * TILE_SUBS
    tile_nsub = jnp.where(t < n_tiles, jnp.minimum(TILE_SUBS, nsub[expert] - within * TILE_SUBS), 0)
    return dict(pos=pos, tile_expert=expert, tile_nsub=tile_nsub.astype(jnp.int32),
                tile_start=start_sub.astype(jnp.int32), used_subs=sub_end[-1:].astype(jnp.int32),
                pad_block=pad_block,
                n_pad_blocks=jnp.sum(pad_block).reshape(1), n_rows=max_subs * SUB)


def _pos_tiles(pos, tp):
    n = pos.shape[0]
    return pos.reshape(n // tp, tp, TOP_K).transpose(0, 2, 1).reshape(n // tp, 1, TOP_K * tp)


def _dispatch_body(pad_ref, npad_ref, pos_ref, x_ref, xs_hbm, zeros, sem):
    i = pl.program_id(0)
    tp, d = x_ref.shape

    def block_copy(b):
        return pltpu.make_async_copy(zeros, xs_hbm.at[pl.ds(pl.multiple_of(b * SUB, SUB), SUB)], sem)

    @pl.when(i == 0)
    def _():
        zeros[...] = jnp.zeros(zeros.shape, zeros.dtype)

        def push(b, carry):
            @pl.when(pad_ref[b] > 0)
            def _():
                block_copy(b).start()
            return carry
        lax.fori_loop(0, pad_ref.shape[0], push, 0)

        def drain(b, carry):
            block_copy(0).wait()
            return carry
        lax.fori_loop(0, npad_ref[0], drain, 0)

    def issue(c, carry):
        for u in range(ISSUE_UNROLL):
            r = c * ISSUE_UNROLL + u
            for k in range(TOP_K):
                pltpu.make_async_copy(x_ref.at[pl.ds(r, 1)], xs_hbm.at[pl.ds(pos_ref[0, k * tp + r], 1)],
                                      sem).start(priority=k % 2)
        return carry
    lax.fori_loop(0, tp // ISSUE_UNROLL, issue, 0)
    for k in range(TOP_K):
        pltpu.make_async_copy(x_ref, xs_hbm.at[pl.ds(0, tp)], sem).wait()


def _dispatch(xn, plan, *, tp):
    n, d = xn.shape
    grid_spec = pltpu.PrefetchScalarGridSpec(
        num_scalar_prefetch=2,
        grid=(n // tp,),
        in_specs=[
            pl.BlockSpec((None, 1, TOP_K * tp), lambda i, pad, npad: (i, 0, 0), memory_space=pltpu.SMEM),
            pl.BlockSpec((tp, d), lambda i, pad, npad: (i, 0)),
        ],
        out_specs=pl.BlockSpec(memory_space=pl.ANY),
        scratch_shapes=[pltpu.VMEM((SUB, d), xn.dtype), pltpu.SemaphoreType.DMA],
    )
    return pl.pallas_call(
        _dispatch_body,
        grid_spec=grid_spec,
        out_shape=jax.ShapeDtypeStruct((plan["n_rows"], d), xn.dtype),
        compiler_params=_params(("arbitrary",)),
        name="dispatch",
    )(plan["pad_block"], plan["n_pad_blocks"], _pos_tiles(plan["pos"], tp), xn)


def _row_copy(src_hbm, dst_vmem, sem, src_row, dst_row):
    return pltpu.make_async_copy(src_hbm.at[pl.ds(src_row, 1)], dst_vmem.at[pl.ds(dst_row, 1)], sem)


def _moe_body(expert_ref, nsub_ref, start_ref, used_ref, xs_hbm, wg_ref, bg_ref, wu_ref, bu_ref, wd_ref, bd_ref,
              ys_hbm, xg, xb, acc, wgb, wub, wdb, gsem, osem, *, n_t, n_f):
    del expert_ref
    t = pl.program_id(0)
    f = pl.program_id(1)
    ns = nsub_ref[t]
    t_next = jnp.minimum(t + 1, n_t - 1)
    ns_next = jnp.where(t + 1 < n_t, nsub_ref[t_next], 0)

    def sub_rows(j):
        return pl.ds(pl.multiple_of(j * SUB, SUB), SUB)

    def in_copy(tile, j):
        src = pl.ds(pl.multiple_of((start_ref[tile] + j) * SUB, SUB), SUB)
        return pltpu.make_async_copy(xs_hbm.at[src], xg.at[sub_rows(j)], gsem)

    def out_copy(tile, j):
        dst = pl.ds(pl.multiple_of((start_ref[tile] + j) * SUB, SUB), SUB)
        return pltpu.make_async_copy(acc.at[sub_rows(j)], ys_hbm.at[dst], osem)

    @pl.when((t == 0) & (f == 0))
    def _():
        used = used_ref[0]
        n_tail = ys_hbm.shape[0] // SUB - used
        xg[0:SUB, :] = jnp.zeros((SUB, xg.shape[1]), F32)

        def tail_copy(j):
            dst = pl.ds(pl.multiple_of((used + j) * SUB, SUB), SUB)
            return pltpu.make_async_copy(xg.at[pl.ds(0, SUB)], ys_hbm.at[dst], osem)

        def push(j, carry):
            tail_copy(j).start()
            return carry
        lax.fori_loop(0, n_tail, push, 0)

        def drain(j, carry):
            tail_copy(j).wait()
            return carry
        lax.fori_loop(0, n_tail, drain, 0)

    @pl.when((f == 0) & (t > 0))
    def _():
        def drain(j, carry):
            out_copy(t - 1, j).wait()
            return carry
        lax.fori_loop(0, nsub_ref[jnp.maximum(t - 1, 0)], drain, 0)

    @pl.when(ns > 0)
    def _():
        @pl.when(f == 0)
        def _():
            @pl.when(t == 0)
            def _():
                def fetch(j, carry):
                    in_copy(t, j).start()
                    return carry
                lax.fori_loop(0, ns, fetch, 0)

            def land(j, carry):
                in_copy(t, j).wait()
                return carry
            lax.fori_loop(0, ns, land, 0)

            def narrow(j, carry):
                xb[sub_rows(j), :] = xg[sub_rows(j), :].astype(BF16)
                acc[sub_rows(j), :] = jnp.broadcast_to(bd_ref[...], (SUB, acc.shape[1]))
                return carry
            lax.fori_loop(0, ns, narrow, 0)

            def prefetch(j, carry):
                in_copy(t_next, j).start()
                return carry
            lax.fori_loop(0, ns_next, prefetch, 0)

        wgb[...] = wg_ref[...].astype(BF16)
        wub[...] = wu_ref[...].astype(BF16)
        wdb[...] = wd_ref[...].astype(BF16)

        def ffn(rows):
            x = xb[rows, :]
            g = jnp.minimum(jnp.dot(x, wgb[...], preferred_element_type=F32) + bg_ref[...], SWIGLU_LIMIT)
            u = jnp.clip(jnp.dot(x, wub[...], preferred_element_type=F32) + bu_ref[...], -SWIGLU_LIMIT, SWIGLU_LIMIT)
            a = (u + 1.0) * g * jax.nn.sigmoid(SWIGLU_ALPHA * g)
            acc[rows, :] += jnp.dot(a.astype(BF16), wdb[...], preferred_element_type=F32)

        longest = FFN_CHUNKS[0]

        def ffn_longest(q, carry):
            ffn(pl.ds(pl.multiple_of(q * (longest * SUB), longest * SUB), longest * SUB))
            return carry
        lax.fori_loop(0, lax.shift_right_logical(ns, longest.bit_length() - 1), ffn_longest, 0)

        for c in FFN_CHUNKS[1:]:
            @pl.when((ns & c) == c)
            def _(c=c):
                start = (ns & ~(2 * c - 1)) * SUB
                ffn(pl.ds(pl.multiple_of(start, c * SUB), c * SUB))

        @pl.when(f == n_f - 1)
        def _():
            def push(j, carry):
                out_copy(t, j).start()
                return carry
            lax.fori_loop(0, ns, push, 0)

            @pl.when(t == n_t - 1)
            def _():
                def drain(j, carry):
                    out_copy(t, j).wait()
                    return carry
                lax.fori_loop(0, ns, drain, 0)


def _moe_ffn(xs, plan, layer, w_gate, b_gate, w_up, b_up, w_down, b_down):
    d = xs.shape[1]
    depth, n_experts, _, d_ff = w_gate.shape
    tile_rows = SUB * TILE_SUBS
    max_tiles = plan["tile_expert"].shape[0]
    n_f = d_ff // FF_BLOCK

    def ff_block(f, nsub_ref, t):
        return jnp.where(nsub_ref[t] > 0, f, n_f - 1)

    col_w = pl.BlockSpec((None, None, d, FF_BLOCK),
                         lambda t, f, e, ns, st, used: (layer, e[t], 0, ff_block(f, ns, t)))
    col_b = pl.BlockSpec((None, None, 1, FF_BLOCK),
                         lambda t, f, e, ns, st, used: (layer, e[t], 0, ff_block(f, ns, t)))
    grid_spec = pltpu.PrefetchScalarGridSpec(
        num_scalar_prefetch=4,
        grid=(max_tiles, n_f),
        in_specs=[
            pl.BlockSpec(memory_space=pl.ANY),
            col_w, col_b, col_w, col_b,
            pl.BlockSpec((None, None, FF_BLOCK, d),
                         lambda t, f, e, ns, st, used: (layer, e[t], ff_block(f, ns, t), 0)),
            pl.BlockSpec((None, None, 1, d), lambda t, f, e, ns, st, used: (layer, e[t], 0, 0)),
        ],
        out_specs=pl.BlockSpec(memory_space=pl.ANY),
        scratch_shapes=[
            pltpu.VMEM((tile_rows, d), F32), pltpu.VMEM((tile_rows, d), BF16), pltpu.VMEM((tile_rows, d), F32),
            pltpu.VMEM((d, FF_BLOCK), BF16), pltpu.VMEM((d, FF_BLOCK), BF16), pltpu.VMEM((FF_BLOCK, d), BF16),
            pltpu.SemaphoreType.DMA, pltpu.SemaphoreType.DMA,
        ],
    )
    return pl.pallas_call(
        functools.partial(_moe_body, n_t=max_tiles, n_f=n_f),
        grid_spec=grid_spec,
        out_shape=jax.ShapeDtypeStruct((plan["n_rows"], d), F32),
        compiler_params=_params(("arbitrary", "arbitrary")),
        name="moe_ffn",
    )(plan["tile_expert"], plan["tile_nsub"], plan["tile_start"], plan["used_subs"], xs,
      w_gate, b_gate.reshape(depth, n_experts, 1, d_ff), w_up, b_up.reshape(depth, n_experts, 1, d_ff),
      w_down, b_down.reshape(depth, n_experts, 1, d))


def _ple_body(pos_ref, next_pos_ref, prob_ref, h_ref, ys_hbm, p_ref, gin_ref, wg_ref, wp_ref, gpost_ref, gfin_ref,
              o_ref, yg, sems, *, final):
    tp = h_ref.shape[0]
    i = pl.program_id(0)
    n_i = pl.num_programs(0)
    slot = i & 1

    def fetch_row(idx_ref, dst_slot, r):
        for k in range(TOP_K):
            _row_copy(ys_hbm, yg.at[dst_slot, k], sems.at[dst_slot], idx_ref[0, k * tp + r], r).start(priority=k % 2)

    @pl.when(i == 0)
    def _():
        def issue(r, carry):
            fetch_row(pos_ref, 0, r)
            return carry
        lax.fori_loop(0, tp, issue, 0, unroll=2)

    for parity in range(2):
        @pl.when((i + 1 < n_i) & (slot == parity))
        def _():
            for r in range(tp):
                fetch_row(next_pos_ref, 1 - parity, r)

    for k in range(TOP_K):
        pltpu.make_async_copy(ys_hbm.at[pl.ds(0, tp)], yg.at[slot, k], sems.at[slot]).wait()

    h2 = h_ref[...]
    for k in range(TOP_K):
        h2 = h2 + prob_ref[:, k:k + 1] * yg[slot, k]
    xn = (_rms(h2) * gin_ref[...]).astype(BF16)
    gate = jax.nn.sigmoid(jnp.dot(xn, wg_ref[...], preferred_element_type=F32))
    ple = _rms(jnp.dot(p_ref[...].astype(BF16), wp_ref[...], preferred_element_type=F32)) * gpost_ref[...]
    h3 = h2 + gate * ple
    if final:
        h3 = _rms(h3) * gfin_ref[...]
    o_ref[...] = h3


def _combine_ple(pos, probs, h1, ys, p, g_in, w_gate_bf16, w_proj_bf16, g_post, g_final, *, tp, final):
    n, d = h1.shape
    d_ple = p.shape[1]
    pos_tiles = _pos_tiles(pos, tp)
    whole = lambda a: pl.BlockSpec(a.shape, lambda i: (0,) * a.ndim)
    row_block = lambda width: pl.BlockSpec((tp, width), lambda i: (i, 0))
    return pl.pallas_call(
        functools.partial(_ple_body, final=final),
        grid=(n // tp,),
        in_specs=[
            pl.BlockSpec((None, 1, TOP_K * tp), lambda i: (i, 0, 0), memory_space=pltpu.SMEM),
            pl.BlockSpec((None, 1, TOP_K * tp), lambda i: (jnp.minimum(i + 1, n // tp - 1), 0, 0),
                         memory_space=pltpu.SMEM),
            row_block(TOP_K), row_block(d), pl.BlockSpec(memory_space=pl.ANY), row_block(d_ple),
            whole(g_in), whole(w_gate_bf16), whole(w_proj_bf16), whole(g_post), whole(g_final),
        ],
        out_specs=row_block(d),
        out_shape=jax.ShapeDtypeStruct((n, d), F32),
        scratch_shapes=[pltpu.VMEM((2, TOP_K, tp, d), F32), pltpu.SemaphoreType.DMA((2,))],
        compiler_params=_params(("arbitrary",)),
        name="combine_ple",
    )(pos_tiles, pos_tiles, probs, h1, ys, p, g_in, w_gate_bf16, w_proj_bf16, g_post, g_final)


def kernel(x, p, g_mix_norm, w_in, conv_w, w_pool, pool_scale, ln_v_g, ln_v_b, w_spatial, b_spatial, g_out,
           w_out, g_ffn_norm, w_router, b_router, w_gate, b_gate, w_up, b_up, w_down, b_down, g_ple_in,
           w_ple_gate, w_ple_proj, g_ple_post, g_final):
    batch, seq, d = x.shape
    depth = w_in.shape[0]
    n = batch * seq
    n_experts = w_router.shape[2]
    row = lambda v: v.reshape(1, -1)

    h = x.reshape(n, d)
    for i in range(depth):
        proj = _inproj(h, row(g_mix_norm[i]), w_in[i].astype(BF16), tm=512, tn=w_in.shape[2] // 2)
        y = _mixers(proj, conv_w[i], w_pool[i], row(pool_scale[i]), row(ln_v_g[i]), row(ln_v_b[i]),
                    w_spatial[i], b_spatial[i].T, row(g_out[i]), seq=seq, t_rows=256)
        w_router_pad = jnp.pad(w_router[i], ((0, 0), (0, ROUTER_LANES - n_experts)))
        w_router_hi = w_router_pad.astype(BF16)
        w_router_lo = (w_router_pad - w_router_hi.astype(F32)).astype(BF16)
        b_router_pad = jnp.pad(row(b_router[i]), ((0, 0), (0, ROUTER_LANES - n_experts)), constant_values=-jnp.inf)
        h1, xn, idx_pad, prob_pad, rank_pad, count_pad = _outproj_router(
            y, h, w_out[i].astype(BF16), row(g_ffn_norm[i]), w_router_hi, w_router_lo, b_router_pad, tm=512, chain=512)
        top_idx, probs = idx_pad[:, :TOP_K], prob_pad[:, :TOP_K]
        plan = _routing_plan(top_idx, rank_pad[:, :TOP_K], count_pad[0, :n_experts].astype(jnp.int32))
        xs = _dispatch(xn, plan, tp=256)
        ys = _moe_ffn(xs, plan, i, w_gate, b_gate, w_up, b_up, w_down, b_down)
        h = _combine_ple(plan["pos"], probs, h1, ys, p[i].reshape(n, -1), row(g_ple_in[i]),
                         w_ple_gate[i].astype(BF16), w_ple_proj[i].astype(BF16), row(g_ple_post[i]),
                         row(g_final), tp=256, final=(i == depth - 1))
    return h.reshape(batch, seq, d)
```

```python
import functools

import jax
import jax.numpy as jnp
from jax import lax
from jax.experimental import pallas as pl
from jax.experimental.pallas import tpu as pltpu

F32 = jnp.float32
BF16 = jnp.bfloat16

EPS = 1e-6
HEAD_DIM = 128
CHUNK = 128
CONV_WIDTH = 3
POOL_WINDOWS = (2, 4, 8, 16)
TOP_K = 4
SWIGLU_LIMIT = 7.0
SWIGLU_ALPHA = 1.702

HALO = 16
ROUTER_LANES = 128
VMEM_LIMIT = 56 * 1024 * 1024

SUB = 128
TILE_SUBS = 10
FF_BLOCK = 256
FFN_CHUNKS = (8, 4, 2, 1)
ISSUE_UNROLL = 8


def _rms(x):
    return x * lax.rsqrt(jnp.mean(x * x, axis=-1, keepdims=True) + EPS)


def _params(semantics):
    return pltpu.CompilerParams(dimension_semantics=semantics, vmem_limit_bytes=VMEM_LIMIT)


def _inproj_body(h_ref, g_ref, w_ref, o_ref):
    xn = _rms(h_ref[...]) * g_ref[...]
    o_ref[...] = jnp.dot(xn.astype(BF16), w_ref[...], preferred_element_type=F32).astype(o_ref.dtype)


def _inproj(h, g, w_bf16, layer, *, tm, tn):
    n, d = h.shape
    d_in = w_bf16.shape[2]
    return pl.pallas_call(
        _inproj_body,
        grid=(d_in // tn, n // tm),
        in_specs=[
            pl.BlockSpec((tm, d), lambda j, i: (i, 0)),
            pl.BlockSpec((1, d), lambda j, i: (0, 0)),
            pl.BlockSpec((None, d, tn), lambda j, i: (layer, 0, j)),
        ],
        out_specs=pl.BlockSpec((tm, tn), lambda j, i: (i, j)),
        out_shape=jax.ShapeDtypeStruct((n, d_in), BF16),
        compiler_params=_params(("arbitrary", "arbitrary")),
        name="inproj",
    )(h, g, w_bf16)


def _mixer_body(proj_ref, halo_ref, convw_ref, wpool_ref, pscale_ref, lng_ref, lnb_ref,
                wsp_ref, bsp_ref, gout_ref, o_ref, za_scr, zb_scr, *, tiles_per_seq, d_conv, d_pool, d_sgu):
    t_rows = proj_ref.shape[0]
    i = pl.program_id(0)
    seq_tile = i % tiles_per_seq
    keep = (seq_tile > 0).astype(F32)

    o_za, o_ba, o_ca = 0, d_conv, 2 * d_conv
    o_zb = 3 * d_conv
    o_u = o_zb + d_pool
    o_v = o_u + d_sgu

    zc = proj_ref[:, o_ca:o_ca + d_conv].astype(F32) * proj_ref[:, o_za:o_za + d_conv].astype(F32)
    zc_halo = (halo_ref[:, o_ca:o_ca + d_conv].astype(F32) * halo_ref[:, o_za:o_za + d_conv].astype(F32)) * keep
    za_scr[0:HALO, :] = zc_halo
    za_scr[HALO:HALO + t_rows, :] = zc
    conv = zc * convw_ref[CONV_WIDTH - 1:CONV_WIDTH, :]
    for k in range(1, CONV_WIDTH):
        conv = conv + za_scr[HALO - k:HALO - k + t_rows, :] * convw_ref[CONV_WIDTH - 1 - k:CONV_WIDTH - k, :]
    y_a = proj_ref[:, o_ba:o_ba + d_conv].astype(F32) * conv

    zb = proj_ref[:, o_zb:o_zb + d_pool].astype(F32)
    zb_scr[0:HALO, :] = halo_ref[:, o_zb:o_zb + d_pool].astype(F32) * keep
    zb_scr[HALO:HALO + t_rows, :] = zb
    t_pos = seq_tile * t_rows + lax.broadcasted_iota(jnp.int32, (t_rows, 1), 0) + 1
    gdim = d_pool // len(POOL_WINDOWS)
    yb_parts = []
    for g, win in enumerate(POOL_WINDOWS):
        c0 = g * gdim
        s = zb[:, c0:c0 + gdim]
        for k in range(1, win):
            s = s + zb_scr[HALO - k:HALO - k + t_rows, c0:c0 + gdim]
        count = jnp.minimum(t_pos, win).astype(F32)
        pooled = s / count - zb[:, c0:c0 + gdim]
        yb_parts.append(jnp.dot(pooled.astype(BF16), wpool_ref[g].astype(BF16), preferred_element_type=F32))
    y_b = jnp.concatenate(yb_parts, axis=-1) * pscale_ref[...]

    v = proj_ref[:, o_v:o_v + d_sgu].astype(F32)
    mu = jnp.mean(v, axis=-1, keepdims=True)
    vc = v - mu
    var = jnp.mean(vc * vc, axis=-1, keepdims=True)
    vn = (vc * lax.rsqrt(var + EPS) * lng_ref[...] + lnb_ref[...]).astype(BF16)
    n_heads = d_sgu // HEAD_DIM
    row = lax.broadcasted_iota(jnp.int32, (CHUNK, CHUNK), 0)
    col = lax.broadcasted_iota(jnp.int32, (CHUNK, CHUNK), 1)
    yc_rows = []
    for c in range(t_rows // CHUNK):
        heads = []
        for hd in range(n_heads):
            ws = jnp.where(row >= col, wsp_ref[hd], 0.0).astype(BF16)
            vch = vn[c * CHUNK:(c + 1) * CHUNK, hd * HEAD_DIM:(hd + 1) * HEAD_DIM]
            heads.append(jnp.dot(ws, vch, preferred_element_type=F32) + bsp_ref[:, hd:hd + 1])
        yc_rows.append(jnp.concatenate(heads, axis=-1))
    y_c = proj_ref[:, o_u:o_u + d_sgu].astype(F32) * jnp.concatenate(yc_rows, axis=0)

    o_ref[:, 0:d_conv] = (_rms(y_a) * gout_ref[:, 0:d_conv]).astype(o_ref.dtype)
    o_ref[:, d_conv:d_conv + d_pool] = (_rms(y_b) * gout_ref[:, d_conv:d_conv + d_pool]).astype(o_ref.dtype)
    o_ref[:, d_conv + d_pool:] = (_rms(y_c) * gout_ref[:, d_conv + d_pool:]).astype(o_ref.dtype)


def _mixers(proj, conv_w, w_pool, pool_scale, ln_g, ln_b, w_spatial, b_spatial_t, g_out, *, seq, t_rows):
    n, d_in = proj.shape
    d_conv = conv_w.shape[1]
    d_pool = pool_scale.shape[1]
    d_sgu = ln_g.shape[1]
    d_mix = g_out.shape[1]
    halo_blocks = t_rows // HALO
    whole = lambda a: pl.BlockSpec(a.shape, lambda i: (0,) * a.ndim)
    body = functools.partial(_mixer_body, tiles_per_seq=seq // t_rows, d_conv=d_conv, d_pool=d_pool, d_sgu=d_sgu)
    return pl.pallas_call(
        body,
        grid=(n // t_rows,),
        in_specs=[
            pl.BlockSpec((t_rows, d_in), lambda i: (i, 0)),
            pl.BlockSpec((HALO, d_in), lambda i: (jnp.maximum(i * halo_blocks - 1, 0), 0)),
            whole(conv_w), whole(w_pool), whole(pool_scale), whole(ln_g), whole(ln_b),
            whole(w_spatial), whole(b_spatial_t), whole(g_out),
        ],
        out_specs=pl.BlockSpec((t_rows, d_mix), lambda i: (i, 0)),
        out_shape=jax.ShapeDtypeStruct((n, d_mix), BF16),
        scratch_shapes=[pltpu.VMEM((HALO + t_rows, d_conv), F32), pltpu.VMEM((HALO + t_rows, d_pool), F32)],
        compiler_params=_params(("arbitrary",)),
        name="mixers",
    )(proj, proj, conv_w, w_pool, pool_scale, ln_g, ln_b, w_spatial, b_spatial_t, g_out)


def _outproj_router_body(y_ref, h_ref, w_ref, g_ref, wrh_ref, wrl_ref, br_ref, h1_ref, xn_ref, idx_ref, prob_ref,
                         rank_ref, count_ref, *, chain):
    @pl.when(pl.program_id(0) == 0)
    def _():
        count_ref[...] = jnp.zeros(count_ref.shape, count_ref.dtype)

    lane = lax.broadcasted_iota(jnp.int32, (chain, ROUTER_LANES), 1)
    earlier = (lax.broadcasted_iota(jnp.int32, (chain, chain), 0)
               > lax.broadcasted_iota(jnp.int32, (chain, chain), 1)).astype(BF16)
    for c in range(h_ref.shape[0] // chain):
        rows = slice(c * chain, (c + 1) * chain)
        h1 = h_ref[rows, :] + jnp.dot(y_ref[rows, :], w_ref[...], preferred_element_type=F32)
        h1_ref[rows, :] = h1
        xn = _rms(h1) * g_ref[...]
        xn_ref[rows, :] = xn
        x_hi = xn.astype(BF16)
        x_lo = (xn - x_hi.astype(F32)).astype(BF16)
        logits = (jnp.dot(x_hi, wrh_ref[...], preferred_element_type=F32)
                  + (jnp.dot(x_lo, wrh_ref[...], preferred_element_type=F32)
                     + jnp.dot(x_hi, wrl_ref[...], preferred_element_type=F32))) + br_ref[...]
        vals, idxs = [], []
        for _ in range(TOP_K):
            m = jnp.max(logits, axis=-1, keepdims=True)
            sel = jnp.min(jnp.where(logits == m, lane, ROUTER_LANES), axis=-1, keepdims=True)
            vals.append(m)
            idxs.append(sel)
            logits = jnp.where(lane == sel, -jnp.inf, logits)
        exps = [jnp.exp(v - vals[0]) for v in vals]
        denom = exps[0]
        for e in exps[1:]:
            denom = denom + e
        chosen = jnp.zeros(lane.shape, F32)
        for k in range(TOP_K):
            chosen = chosen + (lane == idxs[k]).astype(F32)
        before = count_ref[...] + jnp.dot(earlier, chosen.astype(BF16), preferred_element_type=F32)
        count_ref[...] = count_ref[...] + jnp.sum(chosen, axis=0, keepdims=True)
        idx_out = jnp.zeros(lane.shape, jnp.int32)
        prob_out = jnp.zeros(lane.shape, F32)
        rank_out = jnp.zeros(lane.shape, jnp.int32)
        for k in range(TOP_K):
            rank_k = jnp.sum(jnp.where(lane == idxs[k], before, 0.0), axis=-1, keepdims=True).astype(jnp.int32)
            idx_out = jnp.where(lane == k, idxs[k], idx_out)
            prob_out = jnp.where(lane == k, exps[k] / denom, prob_out)
            rank_out = jnp.where(lane == k, rank_k, rank_out)
        idx_ref[rows, :] = idx_out
        prob_ref[rows, :] = prob_out
        rank_ref[rows, :] = rank_out


def _outproj_router(y, h, w_out_bf16, layer, g_ffn, w_router_hi, w_router_lo, b_router_pad, *, tm, chain):
    n, d = h.shape
    whole = lambda a: pl.BlockSpec(a.shape, lambda i: (0,) * a.ndim)
    row_block = lambda width: pl.BlockSpec((tm, width), lambda i: (i, 0))
    lanes = lambda dtype: jax.ShapeDtypeStruct((n, ROUTER_LANES), dtype)
    return pl.pallas_call(
        functools.partial(_outproj_router_body, chain=chain),
        grid=(n // tm,),
        in_specs=[row_block(y.shape[1]), row_block(d),
                  pl.BlockSpec((None,) + w_out_bf16.shape[1:], lambda i: (layer, 0, 0), pipeline_mode=pl.Buffered(1)),
                  whole(g_ffn),
                  whole(w_router_hi), whole(w_router_lo), whole(b_router_pad)],
        out_specs=[row_block(d), row_block(d), row_block(ROUTER_LANES), row_block(ROUTER_LANES),
                   row_block(ROUTER_LANES), pl.BlockSpec((1, ROUTER_LANES), lambda i: (0, 0))],
        out_shape=[jax.ShapeDtypeStruct((n, d), F32), jax.ShapeDtypeStruct((n, d), F32),
                   lanes(jnp.int32), lanes(F32), lanes(jnp.int32),
                   jax.ShapeDtypeStruct((1, ROUTER_LANES), F32)],
        compiler_params=_params(("arbitrary",)),
        name="outproj_router",
    )(y, h, w_out_bf16, g_ffn, w_router_hi, w_router_lo, b_router_pad)


def _routing_plan(top_idx, rank, counts):
    n = top_idx.shape[0]
    n_experts = counts.shape[0]
    max_subs = (n * TOP_K) // SUB + n_experts
    max_tiles = max_subs // TILE_SUBS + n_experts
    nsub = (counts + SUB - 1) // SUB
    sub_end = jnp.cumsum(nsub)
    sub_off = sub_end - nsub
    is_expert = top_idx[:, :, None] == jnp.arange(n_experts, dtype=jnp.int32)
    pos = jnp.sum(jnp.where(is_expert, sub_off * SUB, 0), axis=-1) + rank
    block = jnp.arange(max_subs, dtype=jnp.int32)
    is_last = ((block[:, None] == (sub_end - 1)[None, :]) & (nsub > 0)[None, :]).any(axis=1)
    pad_block = (is_last | (block >= sub_end[-1])).astype(jnp.int32)
    ntile = (nsub + TILE_SUBS - 1) // TILE_SUBS
    tile_end = jnp.cumsum(ntile)
    tile_off = tile_end - ntile
    n_tiles = tile_end[-1]
    t = jnp.arange(max_tiles, dtype=jnp.int32)
    t_eff = jnp.minimum(t, n_tiles - 1)
    expert = jnp.sum((tile_end[None, :] <= t_eff[:, None]).astype(jnp.int32), axis=1)
    within = t_eff - tile_off[expert]
    start_sub = sub_off[expert] + within * TILE_SUBS
    tile_nsub = jnp.where(t < n_tiles, jnp.minimum(TILE_SUBS, nsub[expert] - within * TILE_SUBS), 0)
    return dict(pos=pos, tile_expert=expert, tile_nsub=tile_nsub.astype(jnp.int32),
                tile_start=start_sub.astype(jnp.int32), used_subs=sub_end[-1:].astype(jnp.int32),
                pad_block=pad_block,
                n_pad_blocks=jnp.sum(pad_block).reshape(1), n_rows=max_subs * SUB)


def _pos_tiles(pos, tp):
    n = pos.shape[0]
    return pos.reshape(n // tp, tp, TOP_K).transpose(0, 2, 1).reshape(n // tp, 1, TOP_K * tp)


def _dispatch_body(pad_ref, npad_ref, pos_ref, x_ref, xs_hbm, zeros, sem):
    i = pl.program_id(0)
    tp, d = x_ref.shape

    def block_copy(b):
        return pltpu.make_async_copy(zeros, xs_hbm.at[pl.ds(pl.multiple_of(b * SUB, SUB), SUB)], sem)

    @pl.when(i == 0)
    def _():
        zeros[...] = jnp.zeros(zeros.shape, zeros.dtype)

        def push(b, carry):
            @pl.when(pad_ref[b] > 0)
            def _():
                block_copy(b).start()
            return carry
        lax.fori_loop(0, pad_ref.shape[0], push, 0)

        def drain(b, carry):
            block_copy(0).wait()
            return carry
        lax.fori_loop(0, npad_ref[0], drain, 0)

    def issue(c, carry):
        for u in range(ISSUE_UNROLL):
            r = c * ISSUE_UNROLL + u
            for k in range(TOP_K):
                pltpu.make_async_copy(x_ref.at[pl.ds(r, 1)], xs_hbm.at[pl.ds(pos_ref[0, k * tp + r], 1)],
                                      sem).start(priority=k % 2)
        return carry
    lax.fori_loop(0, tp // ISSUE_UNROLL, issue, 0)
    for k in range(TOP_K):
        pltpu.make_async_copy(x_ref, xs_hbm.at[pl.ds(0, tp)], sem).wait()


def _dispatch(xn, plan, *, tp):
    n, d = xn.shape
    grid_spec = pltpu.PrefetchScalarGridSpec(
        num_scalar_prefetch=2,
        grid=(n // tp,),
        in_specs=[
            pl.BlockSpec((None, 1, TOP_K * tp), lambda i, pad, npad: (i, 0, 0), memory_space=pltpu.SMEM),
            pl.BlockSpec((tp, d), lambda i, pad, npad: (i, 0)),
        ],
        out_specs=pl.BlockSpec(memory_space=pl.ANY),
        scratch_shapes=[pltpu.VMEM((SUB, d), xn.dtype), pltpu.SemaphoreType.DMA],
    )
    return pl.pallas_call(
        _dispatch_body,
        grid_spec=grid_spec,
        out_shape=jax.ShapeDtypeStruct((plan["n_rows"], d), xn.dtype),
        compiler_params=_params(("arbitrary",)),
        name="dispatch",
    )(plan["pad_block"], plan["n_pad_blocks"], _pos_tiles(plan["pos"], tp), xn)


def _row_copy(src_hbm, dst_vmem, sem, src_row, dst_row):
    return pltpu.make_async_copy(src_hbm.at[pl.ds(src_row, 1)], dst_vmem.at[pl.ds(dst_row, 1)], sem)


def _moe_body(expert_ref, nsub_ref, start_ref, used_ref, xs_hbm, wg_ref, bg_ref, wu_ref, bu_ref, wd_ref, bd_ref,
              ys_hbm, xg, xb, acc, wgb, wub, wdb, gsem, osem, *, n_t, n_f):
    del expert_ref
    t = pl.program_id(0)
    f = pl.program_id(1)
    ns = nsub_ref[t]
    t_next = jnp.minimum(t + 1, n_t - 1)
    ns_next = jnp.where(t + 1 < n_t, nsub_ref[t_next], 0)

    def sub_rows(j):
        return pl.ds(pl.multiple_of(j * SUB, SUB), SUB)

    def in_copy(tile, j):
        src = pl.ds(pl.multiple_of((start_ref[tile] + j) * SUB, SUB), SUB)
        return pltpu.make_async_copy(xs_hbm.at[src], xg.at[sub_rows(j)], gsem)

    def out_copy(tile, j):
        dst = pl.ds(pl.multiple_of((start_ref[tile] + j) * SUB, SUB), SUB)
        return pltpu.make_async_copy(acc.at[sub_rows(j)], ys_hbm.at[dst], osem)

    @pl.when((t == 0) & (f == 0))
    def _():
        used = used_ref[0]
        n_tail = ys_hbm.shape[0] // SUB - used
        xg[0:SUB, :] = jnp.zeros((SUB, xg.shape[1]), F32)

        def tail_copy(j):
            dst = pl.ds(pl.multiple_of((used + j) * SUB, SUB), SUB)
            return pltpu.make_async_copy(xg.at[pl.ds(0, SUB)], ys_hbm.at[dst], osem)

        def push(j, carry):
            tail_copy(j).start()
            return carry
        lax.fori_loop(0, n_tail, push, 0)

        def drain(j, carry):
            tail_copy(j).wait()
            return carry
        lax.fori_loop(0, n_tail, drain, 0)

    @pl.when((f == 0) & (t > 0))
    def _():
        def drain(j, carry):
            out_copy(t - 1, j).wait()
            return carry
        lax.fori_loop(0, nsub_ref[jnp.maximum(t - 1, 0)], drain, 0)

    @pl.when(ns > 0)
    def _():
        @pl.when(f == 0)
        def _():
            @pl.when(t == 0)
            def _():
                def fetch(j, carry):
                    in_copy(t, j).start()
                    return carry
                lax.fori_loop(0, ns, fetch, 0)

            def land(j, carry):
                in_copy(t, j).wait()
                return carry
            lax.fori_loop(0, ns, land, 0)

            def narrow(j, carry):
                xb[sub_rows(j), :] = xg[sub_rows(j), :].astype(BF16)
                acc[sub_rows(j), :] = jnp.broadcast_to(bd_ref[...], (SUB, acc.shape[1]))
                return carry
            lax.fori_loop(0, ns, narrow, 0)

            def prefetch(j, carry):
                in_copy(t_next, j).start()
                return carry
            lax.fori_loop(0, ns_next, prefetch, 0)

        wgb[...] = wg_ref[...].astype(BF16)
        wub[...] = wu_ref[...].astype(BF16)
        wdb[...] = wd_ref[...].astype(BF16)

        def ffn(rows):
            x = xb[rows, :]
            g = jnp.minimum(jnp.dot(x, wgb[...], preferred_element_type=F32) + bg_ref[...], SWIGLU_LIMIT)
            u = jnp.clip(jnp.dot(x, wub[...], preferred_element_type=F32) + bu_ref[...], -SWIGLU_LIMIT, SWIGLU_LIMIT)
            a = (u + 1.0) * g * jax.nn.sigmoid(SWIGLU_ALPHA * g)
            acc[rows, :] += jnp.dot(a.astype(BF16), wdb[...], preferred_element_type=F32)

        longest = FFN_CHUNKS[0]

        def ffn_longest(q, carry):
            ffn(pl.ds(pl.multiple_of(q * (longest * SUB), longest * SUB), longest * SUB))
            return carry
        lax.fori_loop(0, lax.shift_right_logical(ns, longest.bit_length() - 1), ffn_longest, 0)

        for c in FFN_CHUNKS[1:]:
            @pl.when((ns & c) == c)
            def _(c=c):
                start = (ns & ~(2 * c - 1)) * SUB
                ffn(pl.ds(pl.multiple_of(start, c * SUB), c * SUB))

        @pl.when(f == n_f - 1)
        def _():
            def push(j, carry):
                out_copy(t, j).start()
                return carry
            lax.fori_loop(0, ns, push, 0)

            @pl.when(t == n_t - 1)
            def _():
                def drain(j, carry):
                    out_copy(t, j).wait()
                    return carry
                lax.fori_loop(0, ns, drain, 0)


def _moe_ffn(xs, plan, layer, w_gate, b_gate, w_up, b_up, w_down, b_down):
    d = xs.shape[1]
    depth, n_experts, _, d_ff = w_gate.shape
    tile_rows = SUB * TILE_SUBS
    max_tiles = plan["tile_expert"].shape[0]
    n_f = d_ff // FF_BLOCK

    def ff_block(f, nsub_ref, t):
        return jnp.where(nsub_ref[t] > 0, f, n_f - 1)

    col_w = pl.BlockSpec((None, None, d, FF_BLOCK),
                         lambda t, f, e, ns, st, used: (layer, e[t], 0, ff_block(f, ns, t)))
    col_b = pl.BlockSpec((None, None, 1, FF_BLOCK),
                         lambda t, f, e, ns, st, used: (layer, e[t], 0, ff_block(f, ns, t)))
    grid_spec = pltpu.PrefetchScalarGridSpec(
        num_scalar_prefetch=4,
        grid=(max_tiles, n_f),
        in_specs=[
            pl.BlockSpec(memory_space=pl.ANY),
            col_w, col_b, col_w, col_b,
            pl.BlockSpec((None, None, FF_BLOCK, d),
                         lambda t, f, e, ns, st, used: (layer, e[t], ff_block(f, ns, t), 0)),
            pl.BlockSpec((None, None, 1, d), lambda t, f, e, ns, st, used: (layer, e[t], 0, 0)),
        ],
        out_specs=pl.BlockSpec(memory_space=pl.ANY),
        scratch_shapes=[
            pltpu.VMEM((tile_rows, d), F32), pltpu.VMEM((tile_rows, d), BF16), pltpu.VMEM((tile_rows, d), F32),
            pltpu.VMEM((d, FF_BLOCK), BF16), pltpu.VMEM((d, FF_BLOCK), BF16), pltpu.VMEM((FF_BLOCK, d), BF16),
            pltpu.SemaphoreType.DMA, pltpu.SemaphoreType.DMA,
        ],
    )
    return pl.pallas_call(
        functools.partial(_moe_body, n_t=max_tiles, n_f=n_f),
        grid_spec=grid_spec,
        out_shape=jax.ShapeDtypeStruct((plan["n_rows"], d), F32),
        compiler_params=_params(("arbitrary", "arbitrary")),
        name="moe_ffn",
    )(plan["tile_expert"], plan["tile_nsub"], plan["tile_start"], plan["used_subs"], xs,
      w_gate, b_gate.reshape(depth, n_experts, 1, d_ff), w_up, b_up.reshape(depth, n_experts, 1, d_ff),
      w_down, b_down.reshape(depth, n_experts, 1, d))


def _ple_body(pos_ref, next_pos_ref, prob_ref, h_ref, ys_hbm, p_ref, gin_ref, wg_ref, wp_ref, gpost_ref, gfin_ref,
              o_ref, yg, sems, *, final):
    tp = h_ref.shape[0]
    i = pl.program_id(0)
    n_i = pl.num_programs(0)
    slot = i & 1

    def fetch_row(idx_ref, dst_slot, r):
        for k in range(TOP_K):
            _row_copy(ys_hbm, yg.at[dst_slot, k], sems.at[dst_slot], idx_ref[0, k * tp + r], r).start(priority=k % 2)

    @pl.when(i == 0)
    def _():
        def issue(r, carry):
            fetch_row(pos_ref, 0, r)
            return carry
        lax.fori_loop(0, tp, issue, 0, unroll=2)

    for parity in range(2):
        @pl.when((i + 1 < n_i) & (slot == parity))
        def _():
            for r in range(tp):
                fetch_row(next_pos_ref, 1 - parity, r)

    for k in range(TOP_K):
        pltpu.make_async_copy(ys_hbm.at[pl.ds(0, tp)], yg.at[slot, k], sems.at[slot]).wait()

    h2 = h_ref[...]
    for k in range(TOP_K):
        h2 = h2 + prob_ref[:, k:k + 1] * yg[slot, k]
    xn = (_rms(h2) * gin_ref[...]).astype(BF16)
    gate = jax.nn.sigmoid(jnp.dot(xn, wg_ref[...], preferred_element_type=F32))
    ple = _rms(jnp.dot(p_ref[...].astype(BF16), wp_ref[...], preferred_element_type=F32)) * gpost_ref[...]
    h3 = h2 + gate * ple
    if final:
        h3 = _rms(h3) * gfin_ref[...]
    o_ref[...] = h3


def _combine_ple(pos, probs, h1, ys, p, layer, g_in, w_gate_bf16, w_proj_bf16, g_post, g_final, *, tp, final):
    n, d = h1.shape
    d_ple = p.shape[2]
    pos_tiles = _pos_tiles(pos, tp)
    whole = lambda a: pl.BlockSpec(a.shape, lambda i: (0,) * a.ndim)
    stacked = lambda a: pl.BlockSpec((None,) + a.shape[1:], lambda i: (layer,) + (0,) * (a.ndim - 1))
    row_block = lambda width: pl.BlockSpec((tp, width), lambda i: (i, 0))
    return pl.pallas_call(
        functools.partial(_ple_body, final=final),
        grid=(n // tp,),
        in_specs=[
            pl.BlockSpec((None, 1, TOP_K * tp), lambda i: (i, 0, 0), memory_space=pltpu.SMEM),
            pl.BlockSpec((None, 1, TOP_K * tp), lambda i: (jnp.minimum(i + 1, n // tp - 1), 0, 0),
                         memory_space=pltpu.SMEM),
            row_block(TOP_K), row_block(d), pl.BlockSpec(memory_space=pl.ANY),
            pl.BlockSpec((None, tp, d_ple), lambda i: (layer, i, 0)),
            whole(g_in), stacked(w_gate_bf16), stacked(w_proj_bf16), whole(g_post), whole(g_final),
        ],
        out_specs=row_block(d),
        out_shape=jax.ShapeDtypeStruct((n, d), F32),
        scratch_shapes=[pltpu.VMEM((2, TOP_K, tp, d), F32), pltpu.SemaphoreType.DMA((2,))],
        compiler_params=_params(("arbitrary",)),
        name="combine_ple",
    )(pos_tiles, pos_tiles, probs, h1, ys, p, g_in, w_gate_bf16, w_proj_bf16, g_post, g_final)


def kernel(x, p, g_mix_norm, w_in, conv_w, w_pool, pool_scale, ln_v_g, ln_v_b, w_spatial, b_spatial, g_out,
           w_out, g_ffn_norm, w_router, b_router, w_gate, b_gate, w_up, b_up, w_down, b_down, g_ple_in,
           w_ple_gate, w_ple_proj, g_ple_post, g_final):
    batch, seq, d = x.shape
    depth = w_in.shape[0]
    n = batch * seq
    n_experts = w_router.shape[2]
    row = lambda v: v.reshape(1, -1)

    w_in_b, w_out_b = w_in.astype(BF16), w_out.astype(BF16)
    w_ple_gate_b, w_ple_proj_b = w_ple_gate.astype(BF16), w_ple_proj.astype(BF16)
    p_rows = p.reshape(depth, n, -1)
    h = x.reshape(n, d)
    for i in range(depth):
        proj = _inproj(h, row(g_mix_norm[i]), w_in_b, i, tm=512, tn=w_in.shape[2] // 2)
        y = _mixers(proj, conv_w[i], w_pool[i], row(pool_scale[i]), row(ln_v_g[i]), row(ln_v_b[i]),
                    w_spatial[i], b_spatial[i].T, row(g_out[i]), seq=seq, t_rows=256)
        w_router_pad = jnp.pad(w_router[i], ((0, 0), (0, ROUTER_LANES - n_experts)))
        w_router_hi = w_router_pad.astype(BF16)
        w_router_lo = (w_router_pad - w_router_hi.astype(F32)).astype(BF16)
        b_router_pad = jnp.pad(row(b_router[i]), ((0, 0), (0, ROUTER_LANES - n_experts)), constant_values=-jnp.inf)
        h1, xn, idx_pad, prob_pad, rank_pad, count_pad = _outproj_router(
            y, h, w_out_b, i, row(g_ffn_norm[i]), w_router_hi, w_router_lo, b_router_pad, tm=512, chain=512)
        top_idx, probs = idx_pad[:, :TOP_K], prob_pad[:, :TOP_K]
        plan = _routing_plan(top_idx, rank_pad[:, :TOP_K], count_pad[0, :n_experts].astype(jnp.int32))
        xs = _dispatch(xn, plan, tp=256)
        ys = _moe_ffn(xs, plan, i, w_gate, b_gate, w_up, b_up, w_down, b_down)
        h = _combine_ple(plan["pos"], probs, h1, ys, p_rows, i, row(g_ple_in[i]),
                         w_ple_gate_b, w_ple_proj_b, row(g_ple_post[i]),
                         row(g_final), tp=256, final=(i == depth - 1))
    return h.reshape(batch, seq, d)
```

```python
import functools

import jax
import jax.numpy as jnp
from jax import lax
from jax.experimental import pallas as pl
from jax.experimental.pallas import tpu as pltpu

F32 = jnp.float32
BF16 = jnp.bfloat16

EPS = 1e-6
HEAD_DIM = 128
CHUNK = 128
CONV_WIDTH = 3
POOL_WINDOWS = (2, 4, 8, 16)
TOP_K = 4
SWIGLU_LIMIT = 7.0
SWIGLU_ALPHA = 1.702

HALO = 16
ROUTER_LANES = 128
VMEM_LIMIT = 56 * 1024 * 1024

SUB = 128
TILE_SUBS = 10
FF_BLOCK = 256
FFN_CHUNKS = (8, 4, 2, 1)
ISSUE_UNROLL = 8


def _rms(x):
    return x * lax.rsqrt(jnp.mean(x * x, axis=-1, keepdims=True) + EPS)


def _params(semantics):
    return pltpu.CompilerParams(dimension_semantics=semantics, vmem_limit_bytes=VMEM_LIMIT)


def _inproj_body(h_ref, g_ref, w_ref, o_ref):
    xn = _rms(h_ref[...]) * g_ref[...]
    o_ref[...] = jnp.dot(xn.astype(BF16), w_ref[...], preferred_element_type=F32).astype(o_ref.dtype)


def _inproj(h, g, w_bf16, layer, *, tm, tn):
    n, d = h.shape
    d_in = w_bf16.shape[2]
    return pl.pallas_call(
        _inproj_body,
        grid=(d_in // tn, n // tm),
        in_specs=[
            pl.BlockSpec((tm, d), lambda j, i: (i, 0)),
            pl.BlockSpec((1, d), lambda j, i: (0, 0)),
            pl.BlockSpec((None, d, tn), lambda j, i: (layer, 0, j)),
        ],
        out_specs=pl.BlockSpec((tm, tn), lambda j, i: (i, j)),
        out_shape=jax.ShapeDtypeStruct((n, d_in), BF16),
        compiler_params=_params(("arbitrary", "arbitrary")),
        name="inproj",
    )(h, g, w_bf16)


def _mixer_body(proj_ref, halo_ref, convw_ref, wpool_ref, pscale_ref, lng_ref, lnb_ref,
                wsp_ref, bsp_ref, gout_ref, o_ref, za_scr, zb_scr, *, tiles_per_seq, d_conv, d_pool, d_sgu):
    t_rows = proj_ref.shape[0]
    i = pl.program_id(0)
    seq_tile = i % tiles_per_seq
    keep = (seq_tile > 0).astype(F32)

    o_za, o_ba, o_ca = 0, d_conv, 2 * d_conv
    o_zb = 3 * d_conv
    o_u = o_zb + d_pool
    o_v = o_u + d_sgu

    zc = proj_ref[:, o_ca:o_ca + d_conv].astype(F32) * proj_ref[:, o_za:o_za + d_conv].astype(F32)
    zc_halo = (halo_ref[:, o_ca:o_ca + d_conv].astype(F32) * halo_ref[:, o_za:o_za + d_conv].astype(F32)) * keep
    za_scr[0:HALO, :] = zc_halo
    za_scr[HALO:HALO + t_rows, :] = zc
    conv = zc * convw_ref[CONV_WIDTH - 1:CONV_WIDTH, :]
    for k in range(1, CONV_WIDTH):
        conv = conv + za_scr[HALO - k:HALO - k + t_rows, :] * convw_ref[CONV_WIDTH - 1 - k:CONV_WIDTH - k, :]
    y_a = proj_ref[:, o_ba:o_ba + d_conv].astype(F32) * conv

    zb = proj_ref[:, o_zb:o_zb + d_pool].astype(F32)
    zb_scr[0:HALO, :] = halo_ref[:, o_zb:o_zb + d_pool].astype(F32) * keep
    zb_scr[HALO:HALO + t_rows, :] = zb
    t_pos = seq_tile * t_rows + lax.broadcasted_iota(jnp.int32, (t_rows, 1), 0) + 1
    gdim = d_pool // len(POOL_WINDOWS)
    yb_parts = []
    for g, win in enumerate(POOL_WINDOWS):
        c0 = g * gdim
        s = zb[:, c0:c0 + gdim]
        for k in range(1, win):
            s = s + zb_scr[HALO - k:HALO - k + t_rows, c0:c0 + gdim]
        count = jnp.minimum(t_pos, win).astype(F32)
        pooled = s / count - zb[:, c0:c0 + gdim]
        yb_parts.append(jnp.dot(pooled.astype(BF16), wpool_ref[g].astype(BF16), preferred_element_type=F32))
    y_b = jnp.concatenate(yb_parts, axis=-1) * pscale_ref[...]

    v = proj_ref[:, o_v:o_v + d_sgu].astype(F32)
    mu = jnp.mean(v, axis=-1, keepdims=True)
    vc = v - mu
    var = jnp.mean(vc * vc, axis=-1, keepdims=True)
    vn = (vc * lax.rsqrt(var + EPS) * lng_ref[...] + lnb_ref[...]).astype(BF16)
    n_heads = d_sgu // HEAD_DIM
    row = lax.broadcasted_iota(jnp.int32, (CHUNK, CHUNK), 0)
    col = lax.broadcasted_iota(jnp.int32, (CHUNK, CHUNK), 1)
    yc_rows = []
    for c in range(t_rows // CHUNK):
        heads = []
        for hd in range(n_heads):
            ws = jnp.where(row >= col, wsp_ref[hd], 0.0).astype(BF16)
            vch = vn[c * CHUNK:(c + 1) * CHUNK, hd * HEAD_DIM:(hd + 1) * HEAD_DIM]
            heads.append(jnp.dot(ws, vch, preferred_element_type=F32) + bsp_ref[:, hd:hd + 1])
        yc_rows.append(jnp.concatenate(heads, axis=-1))
    y_c = proj_ref[:, o_u:o_u + d_sgu].astype(F32) * jnp.concatenate(yc_rows, axis=0)

    o_ref[:, 0:d_conv] = (_rms(y_a) * gout_ref[:, 0:d_conv]).astype(o_ref.dtype)
    o_ref[:, d_conv:d_conv + d_pool] = (_rms(y_b) * gout_ref[:, d_conv:d_conv + d_pool]).astype(o_ref.dtype)
    o_ref[:, d_conv + d_pool:] = (_rms(y_c) * gout_ref[:, d_conv + d_pool:]).astype(o_ref.dtype)


def _mixers(proj, conv_w, w_pool, pool_scale, ln_g, ln_b, w_spatial, b_spatial_t, g_out, *, seq, t_rows):
    n, d_in = proj.shape
    d_conv = conv_w.shape[1]
    d_pool = pool_scale.shape[1]
    d_sgu = ln_g.shape[1]
    d_mix = g_out.shape[1]
    halo_blocks = t_rows // HALO
    whole = lambda a: pl.BlockSpec(a.shape, lambda i: (0,) * a.ndim)
    body = functools.partial(_mixer_body, tiles_per_seq=seq // t_rows, d_conv=d_conv, d_pool=d_pool, d_sgu=d_sgu)
    return pl.pallas_call(
        body,
        grid=(n // t_rows,),
        in_specs=[
            pl.BlockSpec((t_rows, d_in), lambda i: (i, 0)),
            pl.BlockSpec((HALO, d_in), lambda i: (jnp.maximum(i * halo_blocks - 1, 0), 0)),
            whole(conv_w), whole(w_pool), whole(pool_scale), whole(ln_g), whole(ln_b),
            whole(w_spatial), whole(b_spatial_t), whole(g_out),
        ],
        out_specs=pl.BlockSpec((t_rows, d_mix), lambda i: (i, 0)),
        out_shape=jax.ShapeDtypeStruct((n, d_mix), BF16),
        scratch_shapes=[pltpu.VMEM((HALO + t_rows, d_conv), F32), pltpu.VMEM((HALO + t_rows, d_pool), F32)],
        compiler_params=_params(("arbitrary",)),
        name="mixers",
    )(proj, proj, conv_w, w_pool, pool_scale, ln_g, ln_b, w_spatial, b_spatial_t, g_out)


def _outproj_router_body(y_ref, h_ref, w_ref, g_ref, wrh_ref, wrl_ref, br_ref, h1_ref, xn_ref, idx_ref, prob_ref,
                         rank_ref, count_ref, *, chain):
    @pl.when(pl.program_id(0) == 0)
    def _():
        count_ref[...] = jnp.zeros(count_ref.shape, count_ref.dtype)

    lane = lax.broadcasted_iota(jnp.int32, (chain, ROUTER_LANES), 1)
    earlier = (lax.broadcasted_iota(jnp.int32, (chain, chain), 0)
               > lax.broadcasted_iota(jnp.int32, (chain, chain), 1)).astype(BF16)
    for c in range(h_ref.shape[0] // chain):
        rows = slice(c * chain, (c + 1) * chain)
        h1 = h_ref[rows, :] + jnp.dot(y_ref[rows, :], w_ref[...], preferred_element_type=F32)
        h1_ref[rows, :] = h1
        xn = _rms(h1) * g_ref[...]
        xn_ref[rows, :] = xn
        x_hi = xn.astype(BF16)
        x_lo = (xn - x_hi.astype(F32)).astype(BF16)
        logits = (jnp.dot(x_hi, wrh_ref[...], preferred_element_type=F32)
                  + (jnp.dot(x_lo, wrh_ref[...], preferred_element_type=F32)
                     + jnp.dot(x_hi, wrl_ref[...], preferred_element_type=F32))) + br_ref[...]
        vals, idxs = [], []
        for _ in range(TOP_K):
            m = jnp.max(logits, axis=-1, keepdims=True)
            sel = jnp.min(jnp.where(logits == m, lane, ROUTER_LANES), axis=-1, keepdims=True)
            vals.append(m)
            idxs.append(sel)
            logits = jnp.where(lane == sel, -jnp.inf, logits)
        exps = [jnp.exp(v - vals[0]) for v in vals]
        denom = exps[0]
        for e in exps[1:]:
            denom = denom + e
        chosen = jnp.zeros(lane.shape, F32)
        for k in range(TOP_K):
            chosen = chosen + (lane == idxs[k]).astype(F32)
        before = count_ref[...] + jnp.dot(earlier, chosen.astype(BF16), preferred_element_type=F32)
        count_ref[...] = count_ref[...] + jnp.sum(chosen, axis=0, keepdims=True)
        idx_out = jnp.zeros(lane.shape, jnp.int32)
        prob_out = jnp.zeros(lane.shape, F32)
        rank_out = jnp.zeros(lane.shape, jnp.int32)
        for k in range(TOP_K):
            rank_k = jnp.sum(jnp.where(lane == idxs[k], before, 0.0), axis=-1, keepdims=True).astype(jnp.int32)
            idx_out = jnp.where(lane == k, idxs[k], idx_out)
            prob_out = jnp.where(lane == k, exps[k] / denom, prob_out)
            rank_out = jnp.where(lane == k, rank_k, rank_out)
        idx_ref[rows, :] = idx_out
        prob_ref[rows, :] = prob_out
        rank_ref[rows, :] = rank_out


def _outproj_router(y, h, w_out_bf16, layer, g_ffn, w_router_hi, w_router_lo, b_router_pad, *, tm, chain):
    n, d = h.shape
    whole = lambda a: pl.BlockSpec(a.shape, lambda i: (0,) * a.ndim)
    row_block = lambda width: pl.BlockSpec((tm, width), lambda i: (i, 0))
    lanes = lambda dtype: jax.ShapeDtypeStruct((n, ROUTER_LANES), dtype)
    return pl.pallas_call(
        functools.partial(_outproj_router_body, chain=chain),
        grid=(n // tm,),
        in_specs=[row_block(y.shape[1]), row_block(d),
                  pl.BlockSpec((None,) + w_out_bf16.shape[1:], lambda i: (layer, 0, 0), pipeline_mode=pl.Buffered(1)),
                  whole(g_ffn),
                  whole(w_router_hi), whole(w_router_lo), whole(b_router_pad)],
        out_specs=[row_block(d), row_block(d), row_block(ROUTER_LANES), row_block(ROUTER_LANES),
                   row_block(ROUTER_LANES), pl.BlockSpec((1, ROUTER_LANES), lambda i: (0, 0))],
        out_shape=[jax.ShapeDtypeStruct((n, d), F32), jax.ShapeDtypeStruct((n, d), F32),
                   lanes(jnp.int32), lanes(F32), lanes(jnp.int32),
                   jax.ShapeDtypeStruct((1, ROUTER_LANES), F32)],
        compiler_params=_params(("arbitrary",)),
        name="outproj_router",
    )(y, h, w_out_bf16, g_ffn, w_router_hi, w_router_lo, b_router_pad)


def _routing_plan(top_idx, rank, counts):
    n = top_idx.shape[0]
    n_experts = counts.shape[0]
    max_subs = (n * TOP_K) // SUB + n_experts
    max_tiles = max_subs // TILE_SUBS + n_experts
    nsub = (counts + SUB - 1) // SUB
    sub_end = jnp.cumsum(nsub)
    sub_off = sub_end - nsub
    is_expert = top_idx[:, :, None] == jnp.arange(n_experts, dtype=jnp.int32)
    pos = jnp.sum(jnp.where(is_expert, sub_off * SUB, 0), axis=-1) + rank
    block = jnp.arange(max_subs, dtype=jnp.int32)
    is_last = ((block[:, None] == (sub_end - 1)[None, :]) & (nsub > 0)[None, :]).any(axis=1)
    pad_block = (is_last | (block >= sub_end[-1])).astype(jnp.int32)
    ntile = (nsub + TILE_SUBS - 1) // TILE_SUBS
    tile_end = jnp.cumsum(ntile)
    tile_off = tile_end - ntile
    n_tiles = tile_end[-1]
    t = jnp.arange(max_tiles, dtype=jnp.int32)
    t_eff = jnp.minimum(t, n_tiles - 1)
    expert = jnp.sum((tile_end[None, :] <= t_eff[:, None]).astype(jnp.int32), axis=1)
    within = t_eff - tile_off[expert]
    start_sub = sub_off[expert] + within * TILE_SUBS
    tile_nsub = jnp.where(t < n_tiles, jnp.minimum(TILE_SUBS, nsub[expert] - within * TILE_SUBS), 0)
    return dict(pos=pos, tile_expert=expert, tile_nsub=tile_nsub.astype(jnp.int32),
                tile_start=start_sub.astype(jnp.int32), used_subs=sub_end[-1:].astype(jnp.int32),
                pad_block=pad_block,
                n_pad_blocks=jnp.sum(pad_block).reshape(1), n_rows=max_subs * SUB)


def _pos_tiles(pos, tp):
    n = pos.shape[0]
    return pos.reshape(n // tp, tp, TOP_K).transpose(0, 2, 1).reshape(n // tp, 1, TOP_K * tp)


def _dispatch_body(pad_ref, npad_ref, pos_ref, x_ref, xs_hbm, zeros, sem):
    i = pl.program_id(0)
    tp, d = x_ref.shape

    def block_copy(b):
        return pltpu.make_async_copy(zeros, xs_hbm.at[pl.ds(pl.multiple_of(b * SUB, SUB), SUB)], sem)

    @pl.when(i == 0)
    def _():
        zeros[...] = jnp.zeros(zeros.shape, zeros.dtype)

        def push(b, carry):
            @pl.when(pad_ref[b] > 0)
            def _():
                block_copy(b).start()
            return carry
        lax.fori_loop(0, pad_ref.shape[0], push, 0)

        def drain(b, carry):
            block_copy(0).wait()
            return carry
        lax.fori_loop(0, npad_ref[0], drain, 0)

    def issue(c, carry):
        for u in range(ISSUE_UNROLL):
            r = c * ISSUE_UNROLL + u
            for k in range(TOP_K):
                pltpu.make_async_copy(x_ref.at[pl.ds(r, 1)], xs_hbm.at[pl.ds(pos_ref[0, k * tp + r], 1)],
                                      sem).start(priority=k % 2)
        return carry
    lax.fori_loop(0, tp // ISSUE_UNROLL, issue, 0)
    for k in range(TOP_K):
        pltpu.make_async_copy(x_ref, xs_hbm.at[pl.ds(0, tp)], sem).wait()


def _dispatch(xn, plan, *, tp):
    n, d = xn.shape
    grid_spec = pltpu.PrefetchScalarGridSpec(
        num_scalar_prefetch=2,
        grid=(n // tp,),
        in_specs=[
            pl.BlockSpec((None, 1, TOP_K * tp), lambda i, pad, npad: (i, 0, 0), memory_space=pltpu.SMEM),
            pl.BlockSpec((tp, d), lambda i, pad, npad: (i, 0)),
        ],
        out_specs=pl.BlockSpec(memory_space=pl.ANY),
        scratch_shapes=[pltpu.VMEM((SUB, d), xn.dtype), pltpu.SemaphoreType.DMA],
    )
    return pl.pallas_call(
        _dispatch_body,
        grid_spec=grid_spec,
        out_shape=jax.ShapeDtypeStruct((plan["n_rows"], d), xn.dtype),
        compiler_params=_params(("arbitrary",)),
        name="dispatch",
    )(plan["pad_block"], plan["n_pad_blocks"], _pos_tiles(plan["pos"], tp), xn)


def _row_copy(src_hbm, dst_vmem, sem, src_row, dst_row):
    return pltpu.make_async_copy(src_hbm.at[pl.ds(src_row, 1)], dst_vmem.at[pl.ds(dst_row, 1)], sem)


def _moe_body(expert_ref, nsub_ref, start_ref, used_ref, xs_hbm, wg_ref, bg_ref, wu_ref, bu_ref, wd_ref, bd_ref,
              ys_hbm, xg, xb, acc, wgb, wub, wdb, gsem, osem, *, n_t, n_f):
    del expert_ref
    t = pl.program_id(0)
    f = pl.program_id(1)
    ns = nsub_ref[t]
    t_next = jnp.minimum(t + 1, n_t - 1)
    ns_next = jnp.where(t + 1 < n_t, nsub_ref[t_next], 0)

    def sub_rows(j):
        return pl.ds(pl.multiple_of(j * SUB, SUB), SUB)

    def in_copy(tile, j):
        src = pl.ds(pl.multiple_of((start_ref[tile] + j) * SUB, SUB), SUB)
        return pltpu.make_async_copy(xs_hbm.at[src], xg.at[sub_rows(j)], gsem)

    def out_copy(tile, j):
        dst = pl.ds(pl.multiple_of((start_ref[tile] + j) * SUB, SUB), SUB)
        return pltpu.make_async_copy(acc.at[sub_rows(j)], ys_hbm.at[dst], osem)

    @pl.when((t == 0) & (f == 0))
    def _():
        used = used_ref[0]
        n_tail = ys_hbm.shape[0] // SUB - used
        xg[0:SUB, :] = jnp.zeros((SUB, xg.shape[1]), F32)

        def tail_copy(j):
            dst = pl.ds(pl.multiple_of((used + j) * SUB, SUB), SUB)
            return pltpu.make_async_copy(xg.at[pl.ds(0, SUB)], ys_hbm.at[dst], osem)

        def push(j, carry):
            tail_copy(j).start()
            return carry
        lax.fori_loop(0, n_tail, push, 0)

        def drain(j, carry):
            tail_copy(j).wait()
            return carry
        lax.fori_loop(0, n_tail, drain, 0)

    @pl.when((f == 0) & (t > 0))
    def _():
        def drain(j, carry):
            out_copy(t - 1, j).wait()
            return carry
        lax.fori_loop(0, nsub_ref[jnp.maximum(t - 1, 0)], drain, 0)

    @pl.when(ns > 0)
    def _():
        @pl.when(f == 0)
        def _():
            @pl.when(t == 0)
            def _():
                def fetch(j, carry):
                    in_copy(t, j).start()
                    return carry
                lax.fori_loop(0, ns, fetch, 0)

            def land(j, carry):
                in_copy(t, j).wait()
                return carry
            lax.fori_loop(0, ns, land, 0)

            def narrow(j, carry):
                xb[sub_rows(j), :] = xg[sub_rows(j), :].astype(BF16)
                acc[sub_rows(j), :] = jnp.broadcast_to(bd_ref[...], (SUB, acc.shape[1]))
                return carry
            lax.fori_loop(0, ns, narrow, 0)

            def prefetch(j, carry):
                in_copy(t_next, j).start()
                return carry
            lax.fori_loop(0, ns_next, prefetch, 0)

        wgb[...] = wg_ref[...].astype(BF16)
        wub[...] = wu_ref[...].astype(BF16)
        wdb[...] = wd_ref[...].astype(BF16)

        def ffn(rows):
            x = xb[rows, :]
            g = jnp.minimum(jnp.dot(x, wgb[...], preferred_element_type=F32) + bg_ref[...], SWIGLU_LIMIT)
            u = jnp.clip(jnp.dot(x, wub[...], preferred_element_type=F32) + bu_ref[...], -SWIGLU_LIMIT, SWIGLU_LIMIT)
            a = (u + 1.0) * g * jax.nn.sigmoid(SWIGLU_ALPHA * g)
            acc[rows, :] += jnp.dot(a.astype(BF16), wdb[...], preferred_element_type=F32)

        longest = FFN_CHUNKS[0]

        def ffn_longest(q, carry):
            ffn(pl.ds(pl.multiple_of(q * (longest * SUB), longest * SUB), longest * SUB))
            return carry
        lax.fori_loop(0, lax.shift_right_logical(ns, longest.bit_length() - 1), ffn_longest, 0)

        for c in FFN_CHUNKS[1:]:
            @pl.when((ns & c) == c)
            def _(c=c):
                start = (ns & ~(2 * c - 1)) * SUB
                ffn(pl.ds(pl.multiple_of(start, c * SUB), c * SUB))

        @pl.when(f == n_f - 1)
        def _():
            def push(j, carry):
                out_copy(t, j).start()
                return carry
            lax.fori_loop(0, ns, push, 0)

            @pl.when(t == n_t - 1)
            def _():
                def drain(j, carry):
                    out_copy(t, j).wait()
                    return carry
                lax.fori_loop(0, ns, drain, 0)


def _moe_ffn(xs, plan, layer, w_gate, b_gate, w_up, b_up, w_down, b_down):
    d = xs.shape[1]
    depth, n_experts, _, d_ff = w_gate.shape
    tile_rows = SUB * TILE_SUBS
    max_tiles = plan["tile_expert"].shape[0]
    n_f = d_ff // FF_BLOCK

    def ff_block(f, nsub_ref, t):
        return jnp.where(nsub_ref[t] > 0, f, n_f - 1)

    col_w = pl.BlockSpec((None, None, d, FF_BLOCK),
                         lambda t, f, e, ns, st, used: (layer, e[t], 0, ff_block(f, ns, t)))
    col_b = pl.BlockSpec((None, None, 1, FF_BLOCK),
                         lambda t, f, e, ns, st, used: (layer, e[t], 0, ff_block(f, ns, t)))
    grid_spec = pltpu.PrefetchScalarGridSpec(
        num_scalar_prefetch=4,
        grid=(max_tiles, n_f),
        in_specs=[
            pl.BlockSpec(memory_space=pl.ANY),
            col_w, col_b, col_w, col_b,
            pl.BlockSpec((None, None, FF_BLOCK, d),
                         lambda t, f, e, ns, st, used: (layer, e[t], ff_block(f, ns, t), 0)),
            pl.BlockSpec((None, None, 1, d), lambda t, f, e, ns, st, used: (layer, e[t], 0, 0)),
        ],
        out_specs=pl.BlockSpec(memory_space=pl.ANY),
        scratch_shapes=[
            pltpu.VMEM((tile_rows, d), F32), pltpu.VMEM((tile_rows, d), BF16), pltpu.VMEM((tile_rows, d), F32),
            pltpu.VMEM((d, FF_BLOCK), BF16), pltpu.VMEM((d, FF_BLOCK), BF16), pltpu.VMEM((FF_BLOCK, d), BF16),
            pltpu.SemaphoreType.DMA, pltpu.SemaphoreType.DMA,
        ],
    )
    return pl.pallas_call(
        functools.partial(_moe_body, n_t=max_tiles, n_f=n_f),
        grid_spec=grid_spec,
        out_shape=jax.ShapeDtypeStruct((plan["n_rows"], d), F32),
        compiler_params=_params(("arbitrary", "arbitrary")),
        name="moe_ffn",
    )(plan["tile_expert"], plan["tile_nsub"], plan["tile_start"], plan["used_subs"], xs,
      w_gate, b_gate.reshape(depth, n_experts, 1, d_ff), w_up, b_up.reshape(depth, n_experts, 1, d_ff),
      w_down, b_down.reshape(depth, n_experts, 1, d))


def _ple_body(pos_ref, next_pos_ref, prob_ref, h_ref, ys_hbm, p_ref, gin_ref, wg_ref, wp_ref, gpost_ref, gfin_ref,
              o_ref, yg, sems, *, final):
    tp = h_ref.shape[0]
    i = pl.program_id(0)
    n_i = pl.num_programs(0)
    slot = i & 1

    def fetch_row(idx_ref, dst_slot, r):
        for k in range(TOP_K):
            _row_copy(ys_hbm, yg.at[dst_slot, k], sems.at[dst_slot], idx_ref[0, k * tp + r], r).start(priority=k % 2)

    @pl.when(i == 0)
    def _():
        def issue(r, carry):
            fetch_row(pos_ref, 0, r)
            return carry
        lax.fori_loop(0, tp, issue, 0, unroll=2)

    for parity in range(2):
        @pl.when((i + 1 < n_i) & (slot == parity))
        def _():
            for r in range(tp):
                fetch_row(next_pos_ref, 1 - parity, r)

    for k in range(TOP_K):
        pltpu.make_async_copy(ys_hbm.at[pl.ds(0, tp)], yg.at[slot, k], sems.at[slot]).wait()

    h2 = h_ref[...]
    for k in range(TOP_K):
        h2 = h2 + prob_ref[:, k:k + 1] * yg[slot, k]
    xn = (_rms(h2) * gin_ref[...]).astype(BF16)
    gate = jax.nn.sigmoid(jnp.dot(xn, wg_ref[...], preferred_element_type=F32))
    ple = _rms(jnp.dot(p_ref[...].astype(BF16), wp_ref[...], preferred_element_type=F32)) * gpost_ref[...]
    h3 = h2 + gate * ple
    if final:
        h3 = _rms(h3) * gfin_ref[...]
    o_ref[...] = h3


def _combine_ple(pos, probs, h1, ys, p, layer, g_in, w_gate_bf16, w_proj_bf16, g_post, g_final, *, tp, final):
    n, d = h1.shape
    d_ple = p.shape[2]
    pos_tiles = _pos_tiles(pos, tp)
    whole = lambda a: pl.BlockSpec(a.shape, lambda i: (0,) * a.ndim)
    stacked = lambda a: pl.BlockSpec((None,) + a.shape[1:], lambda i: (layer,) + (0,) * (a.ndim - 1))
    row_block = lambda width: pl.BlockSpec((tp, width), lambda i: (i, 0))
    return pl.pallas_call(
        functools.partial(_ple_body, final=final),
        grid=(n // tp,),
        in_specs=[
            pl.BlockSpec((None, 1, TOP_K * tp), lambda i: (i, 0, 0), memory_space=pltpu.SMEM),
            pl.BlockSpec((None, 1, TOP_K * tp), lambda i: (jnp.minimum(i + 1, n // tp - 1), 0, 0),
                         memory_space=pltpu.SMEM),
            row_block(TOP_K), row_block(d), pl.BlockSpec(memory_space=pl.ANY),
            pl.BlockSpec((None, tp, d_ple), lambda i: (layer, i, 0)),
            whole(g_in), stacked(w_gate_bf16), stacked(w_proj_bf16), whole(g_post), whole(g_final),
        ],
        out_specs=row_block(d),
        out_shape=jax.ShapeDtypeStruct((n, d), F32),
        scratch_shapes=[pltpu.VMEM((2, TOP_K, tp, d), F32), pltpu.SemaphoreType.DMA((2,))],
        compiler_params=_params(("arbitrary",)),
        name="combine_ple",
    )(pos_tiles, pos_tiles, probs, h1, ys, p, g_in, w_gate_bf16, w_proj_bf16, g_post, g_final)


def kernel(x, p, g_mix_norm, w_in, conv_w, w_pool, pool_scale, ln_v_g, ln_v_b, w_spatial, b_spatial, g_out,
           w_out, g_ffn_norm, w_router, b_router, w_gate, b_gate, w_up, b_up, w_down, b_down, g_ple_in,
           w_ple_gate, w_ple_proj, g_ple_post, g_final):
    batch, seq, d = x.shape
    depth = w_in.shape[0]
    n = batch * seq
    n_experts = w_router.shape[2]
    row = lambda v: v.reshape(1, -1)

    w_in_b, w_out_b = w_in.astype(BF16), w_out.astype(BF16)
    w_ple_gate_b, w_ple_proj_b = w_ple_gate.astype(BF16), w_ple_proj.astype(BF16)
    p_rows = p.reshape(depth, n, -1)
    h = x.reshape(n, d)
    for i in range(depth):
        proj = _inproj(h, row(g_mix_norm[i]), w_in_b, i, tm=512, tn=w_in.shape[2] // 2)
        y = _mixers(proj, conv_w[i], w_pool[i], row(pool_scale[i]), row(ln_v_g[i]), row(ln_v_b[i]),
                    w_spatial[i], b_spatial[i].T, row(g_out[i]), seq=seq, t_rows=256)
        w_router_pad = jnp.pad(w_router[i], ((0, 0), (0, ROUTER_LANES - n_experts)))
        w_router_hi = w_router_pad.astype(BF16)
        w_router_lo = (w_router_pad - w_router_hi.astype(F32)).astype(BF16)
        b_router_pad = jnp.pad(row(b_router[i]), ((0, 0), (0, ROUTER_LANES - n_experts)), constant_values=-jnp.inf)
        h1, xn, idx_pad, prob_pad, rank_pad, count_pad = _outproj_router(
            y, h, w_out_b, i, row(g_ffn_norm[i]), w_router_hi, w_router_lo, b_router_pad, tm=512, chain=512)
        top_idx, probs = idx_pad[:, :TOP_K], prob_pad[:, :TOP_K]
        plan = _routing_plan(top_idx, rank_pad[:, :TOP_K], count_pad[0, :n_experts].astype(jnp.int32))
        xs = _dispatch(xn, plan, tp=1024)
        ys = _moe_ffn(xs, plan, i, w_gate, b_gate, w_up, b_up, w_down, b_down)
        h = _combine_ple(plan["pos"], probs, h1, ys, p_rows, i, row(g_ple_in[i]),
                         w_ple_gate_b, w_ple_proj_b, row(g_ple_post[i]),
                         row(g_final), tp=256, final=(i == depth - 1))
    return h.reshape(batch, seq, d)
```
